```python
import math
import jax, jax.numpy as jnp
from jax import lax
import numpy as np

D_MODEL = 1024
BATCH = 8
SEQ = 4096
DEPTH = 2

CHUNK = 64
N_MIXERS = 2
N_SSD_LAYERS = (DEPTH + 1) // 2
N_MLA_LAYERS = DEPTH // 2

SSD_EXPAND = 2
SSD_D_INNER = SSD_EXPAND * D_MODEL
SSD_HEAD_DIM = 64
SSD_N_HEADS = SSD_D_INNER // SSD_HEAD_DIM
SSD_N_GROUPS = 8
SSD_HEADS_PER_GROUP = SSD_N_HEADS // SSD_N_GROUPS
SSD_D_STATE = 128
SSD_CONV_W = 4
SSD_BC_DIM = SSD_N_GROUPS * SSD_D_STATE
SSD_CONV_DIM = SSD_D_INNER + 2 * SSD_BC_DIM
SSD_IN_DIM = SSD_D_INNER + SSD_CONV_DIM + SSD_N_HEADS

MLA_N_HEADS = 16
MLA_Q_RANK = 384
MLA_KV_RANK = 256
MLA_NOPE = 64
MLA_ROPE = 32
MLA_V = 64
MLA_DOWN_DIM = MLA_Q_RANK + MLA_KV_RANK + MLA_ROPE
ROPE_THETA = 10000.0
Q_BLOCK = 128

N_EXPERTS = 16
N_EXPERT_GROUPS = 4
EXPERTS_PER_GROUP = N_EXPERTS // N_EXPERT_GROUPS
TOP_K = 2
D_FF_EXPERT = 512

DEEPNORM_ALPHA = (2.0 * DEPTH) ** 0.25
DEEPNORM_BETA = (8.0 * DEPTH) ** -0.25
LN_EPS = 1e-5
RMS_EPS = 1e-6

kernel_name = "hybrid_ssd_mla_grouped_moe_deepnorm"

F32 = jnp.float32


def layer_norm(x, g, b):
    xf = x.astype(F32)
    mu = jnp.mean(xf, -1, keepdims=True)
    var = jnp.mean(jnp.square(xf - mu), -1, keepdims=True)
    return ((xf - mu) * lax.rsqrt(var + LN_EPS) * g.astype(F32) + b.astype(F32)).astype(x.dtype)


def rms_norm(x, w):
    xf = x.astype(F32)
    y = xf * lax.rsqrt(jnp.mean(xf * xf, -1, keepdims=True) + RMS_EPS)
    return (y * w.astype(F32)).astype(x.dtype)


def causal_depthwise_conv(x, w, b):
    c = x.shape[-1]
    y = lax.conv_general_dilated(x, w[:, None, :], window_strides=(1,),
                                 padding=[(SSD_CONV_W - 1, 0)],
                                 dimension_numbers=("NWC", "WIO", "NWC"),
                                 feature_group_count=c)
    return y + b


def ssd_chunked_scan(xdt, a, b, c):
    bsz, s = xdt.shape[:2]
    nc = s // CHUNK

    def to_chunks(t):
        return jnp.moveaxis(t.astype(F32).reshape(bsz, nc, CHUNK, *t.shape[2:]), 1, 0)

    xs = (to_chunks(xdt), to_chunks(a), to_chunks(b), to_chunks(c))
    causal = jnp.tril(jnp.ones((CHUNK, CHUNK), bool))[None, :, :, None, None]

    def step(state, inp):
        x_c, a_c, b_c, c_c = inp
        a_cum = jnp.cumsum(a_c, axis=1)
        seg = a_cum[:, :, None] - a_cum[:, None, :]
        decay = jnp.exp(jnp.where(causal, seg, -jnp.inf))
        scores = jnp.einsum("blgn,bsgn->blsg", c_c, b_c)
        y_diag = jnp.einsum("blsg,blsgr,bsgrp->blgrp", scores, decay, x_c)
        y_off = jnp.einsum("blgn,bgrpn,blgr->blgrp", c_c, state, jnp.exp(a_cum))
        to_end = jnp.exp(a_cum[:, -1:] - a_cum)
        new_state = (state * jnp.exp(a_cum[:, -1])[..., None, None]
                     + jnp.einsum("blgn,blgr,blgrp->bgrpn", b_c, to_end, x_c))
        return new_state, y_diag + y_off

    state0 = jnp.zeros((bsz, SSD_N_GROUPS, SSD_HEADS_PER_GROUP, SSD_HEAD_DIM, SSD_D_STATE), F32)
    _, ys = lax.scan(step, state0, xs)
    return jnp.moveaxis(ys, 0, 1).reshape(xdt.shape)


def ssd_mixer(x, w_in, conv_w, conv_b, dt_bias, a_log, d_skip, norm_w, w_out):
    bsz, s, _ = x.shape
    g, r, p, n = SSD_N_GROUPS, SSD_HEADS_PER_GROUP, SSD_HEAD_DIM, SSD_D_STATE
    proj = x @ w_in
    z, xbc, dt = jnp.split(proj, [SSD_D_INNER, SSD_D_INNER + SSD_CONV_DIM], axis=-1)
    xbc = jax.nn.silu(causal_depthwise_conv(xbc, conv_w, conv_b))
    xs, bm, cm = jnp.split(xbc, [SSD_D_INNER, SSD_D_INNER + SSD_BC_DIM], axis=-1)
    xs = xs.astype(F32).reshape(bsz, s, g, r, p)
    bm = bm.reshape(bsz, s, g, n)
    cm = cm.reshape(bsz, s, g, n)
    dt = jax.nn.softplus(dt.astype(F32) + dt_bias.astype(F32)).reshape(bsz, s, g, r)
    a = -jnp.exp(a_log.astype(F32)).reshape(g, r)
    y = ssd_chunked_scan(xs * dt[..., None], dt * a, bm, cm)
    y = y + xs * d_skip.astype(F32).reshape(g, r, 1)
    y = (y.reshape(bsz, s, SSD_D_INNER) * jax.nn.silu(z.astype(F32))).reshape(bsz, s, g, -1)
    y = y * lax.rsqrt(jnp.mean(y * y, -1, keepdims=True) + RMS_EPS)
    y = y.reshape(bsz, s, SSD_D_INNER) * norm_w.astype(F32)
    return y.astype(x.dtype) @ w_out


def rope_tables(s):
    inv = ROPE_THETA ** (-jnp.arange(0, MLA_ROPE, 2, dtype=F32) / MLA_ROPE)
    ang = jnp.arange(s, dtype=F32)[:, None] * inv[None, :]
    return jnp.cos(ang), jnp.sin(ang)


def apply_rope(t, cos, sin):
    tf = t.astype(F32)
    t1, t2 = jnp.split(tf, 2, axis=-1)
    extra = t.ndim - 3
    cos = cos.reshape(cos.shape[0], *([1] * extra), cos.shape[1])
    sin = sin.reshape(sin.shape[0], *([1] * extra), sin.shape[1])
    return jnp.concatenate([t1 * cos - t2 * sin, t1 * sin + t2 * cos], -1).astype(t.dtype)


def mla_mixer(x, w_down, q_norm_w, w_uq, kv_norm_w, w_ukv, w_out):
    bsz, s, _ = x.shape
    h = MLA_N_HEADS
    down = x @ w_down
    c_q, c_kv, k_rope = jnp.split(down, [MLA_Q_RANK, MLA_Q_RANK + MLA_KV_RANK], axis=-1)
    q = (rms_norm(c_q, q_norm_w) @ w_uq).reshape(bsz, s, h, MLA_NOPE + MLA_ROPE)
    kv = (rms_norm(c_kv, kv_norm_w) @ w_ukv).reshape(bsz, s, h, MLA_NOPE + MLA_V)
    q_nope, q_rope = jnp.split(q, [MLA_NOPE], axis=-1)
    k_nope, v = jnp.split(kv, [MLA_NOPE], axis=-1)
    cos, sin = rope_tables(s)
    q_rope = apply_rope(q_rope, cos, sin)
    k_rope = apply_rope(k_rope, cos, sin)
    scale = 1.0 / math.sqrt(MLA_NOPE + MLA_ROPE)
    nblk = s // Q_BLOCK
    qn_blocks = jnp.moveaxis(q_nope.reshape(bsz, nblk, Q_BLOCK, h, MLA_NOPE), 1, 0)
    qr_blocks = jnp.moveaxis(q_rope.reshape(bsz, nblk, Q_BLOCK, h, MLA_ROPE), 1, 0)
    k_chunk = jnp.arange(s) // CHUNK

    def attend(args):
        qn, qr, blk = args
        q_chunk = (blk * Q_BLOCK + jnp.arange(Q_BLOCK)) // CHUNK
        mask = k_chunk[None, :] <= q_chunk[:, None]
        scores = (jnp.einsum("bqhd,bkhd->bhqk", qn, k_nope, preferred_element_type=F32)
                  + jnp.einsum("bqhr,bkr->bhqk", qr, k_rope, preferred_element_type=F32)) * scale
        probs = jax.nn.softmax(jnp.where(mask, scores, -jnp.inf), axis=-1)
        return jnp.einsum("bhqk,bkhd->bqhd", probs.astype(v.dtype), v)

    out = lax.map(attend, (qn_blocks, qr_blocks, jnp.arange(nblk)))
    out = jnp.moveaxis(out, 0, 1).reshape(bsz, s, h * MLA_V)
    return out @ w_out


def grouped_moe(x, router_w, router_bias, w_gate, w_up, w_down):
    bsz, s, d = x.shape
    t = x.reshape(-1, d)
    affinity = jax.nn.sigmoid(jnp.dot(t, router_w, preferred_element_type=F32))
    sel = (affinity + router_bias.astype(F32)).reshape(-1, N_EXPERT_GROUPS, EXPERTS_PER_GROUP)
    group_score = lax.top_k(sel, TOP_K)[0].sum(-1)
    best_group = jnp.argmax(group_score, axis=-1)
    in_group = jnp.take_along_axis(sel, best_group[:, None, None], axis=1)[:, 0]
    _, local_idx = lax.top_k(in_group, TOP_K)
    expert_idx = best_group[:, None] * EXPERTS_PER_GROUP + local_idx
    w_sel = jnp.take_along_axis(affinity, expert_idx, axis=-1)
    w_sel = w_sel / jnp.sum(w_sel, -1, keepdims=True)
    gates = jnp.sum(jax.nn.one_hot(expert_idx, N_EXPERTS, dtype=F32) * w_sel[..., None], axis=1)
    out = jnp.zeros((t.shape[0], d), F32)
    for e in range(N_EXPERTS):
        hdn = jax.nn.silu(t @ w_gate[e]) * (t @ w_up[e])
        out = out + gates[:, e:e + 1] * (hdn @ w_down[e])
    return out.astype(x.dtype).reshape(bsz, s, d)


def setup_inputs(seed: int = 0) -> dict:
    key = jax.random.key(seed)
    ks = jax.random.split(key, 32)
    nrm = jax.random.normal
    Ls, Lm = N_SSD_LAYERS, N_MLA_LAYERS
    dt0 = jnp.exp(jax.random.uniform(ks[4], (Ls, SSD_N_HEADS), minval=math.log(1e-3), maxval=math.log(1e-1)))
    return {
        "x": nrm(ks[0], (BATCH, SEQ, D_MODEL), F32),
        "ssd_w_in": nrm(ks[1], (Ls, D_MODEL, SSD_IN_DIM), F32) * D_MODEL ** -0.5,
        "ssd_conv_w": nrm(ks[2], (Ls, SSD_CONV_W, SSD_CONV_DIM), F32) * SSD_CONV_W ** -0.5,
        "ssd_conv_b": 0.01 * nrm(ks[3], (Ls, SSD_CONV_DIM), F32),
        "ssd_dt_bias": dt0 + jnp.log(-jnp.expm1(-dt0)),
        "ssd_a_log": jnp.log(jax.random.uniform(ks[5], (Ls, SSD_N_HEADS), minval=1.0, maxval=16.0)),
        "ssd_d": 1.0 + 0.1 * nrm(ks[6], (Ls, SSD_N_HEADS), F32),
        "ssd_norm_w": 1.0 + 0.05 * nrm(ks[7], (Ls, SSD_D_INNER), F32),
        "ssd_w_out": nrm(ks[8], (Ls, SSD_D_INNER, D_MODEL), F32) * SSD_D_INNER ** -0.5 * DEEPNORM_BETA,
        "mla_w_down": nrm(ks[9], (Lm, D_MODEL, MLA_DOWN_DIM), F32) * D_MODEL ** -0.5,
        "mla_q_norm": 1.0 + 0.05 * nrm(ks[10], (Lm, MLA_Q_RANK), F32),
        "mla_w_uq": nrm(ks[11], (Lm, MLA_Q_RANK, MLA_N_HEADS * (MLA_NOPE + MLA_ROPE)), F32) * MLA_Q_RANK ** -0.5,
        "mla_kv_norm": 1.0 + 0.05 * nrm(ks[12], (Lm, MLA_KV_RANK), F32),
        "mla_w_ukv": nrm(ks[13], (Lm, MLA_KV_RANK, MLA_N_HEADS * (MLA_NOPE + MLA_V)), F32) * MLA_KV_RANK ** -0.5,
        "mla_w_out": nrm(ks[14], (Lm, MLA_N_HEADS * MLA_V, D_MODEL), F32) * (MLA_N_HEADS * MLA_V) ** -0.5 * DEEPNORM_BETA,
        "router_w": nrm(ks[15], (D_MODEL, N_EXPERTS), F32) * D_MODEL ** -0.5,
        "router_bias": 0.01 * nrm(ks[16], (N_EXPERTS,), F32),
        "moe_w_gate": nrm(ks[17], (DEPTH, N_EXPERTS, D_MODEL, D_FF_EXPERT), F32) * D_MODEL ** -0.5,
        "moe_w_up": nrm(ks[18], (DEPTH, N_EXPERTS, D_MODEL, D_FF_EXPERT), F32) * D_MODEL ** -0.5,
        "moe_w_down": nrm(ks[19], (DEPTH, N_EXPERTS, D_FF_EXPERT, D_MODEL), F32) * D_FF_EXPERT ** -0.5 * DEEPNORM_BETA,
        "ln_mix_g": 1.0 + 0.05 * nrm(ks[20], (DEPTH, D_MODEL), F32),
        "ln_mix_b": 0.01 * nrm(ks[21], (DEPTH, D_MODEL), F32),
        "ln_ffn_g": 1.0 + 0.05 * nrm(ks[22], (DEPTH, D_MODEL), F32),
        "ln_ffn_b": 0.01 * nrm(ks[23], (DEPTH, D_MODEL), F32),
    }


def reference(x, ssd_w_in, ssd_conv_w, ssd_conv_b, ssd_dt_bias, ssd_a_log, ssd_d, ssd_norm_w, ssd_w_out,
              mla_w_down, mla_q_norm, mla_w_uq, mla_kv_norm, mla_w_ukv, mla_w_out,
              router_w, router_bias, moe_w_gate, moe_w_up, moe_w_down,
              ln_mix_g, ln_mix_b, ln_ffn_g, ln_ffn_b):
    h = x
    for i in range(DEPTH):
        j = i // N_MIXERS
        if i % N_MIXERS == 0:
            m = ssd_mixer(h, ssd_w_in[j], ssd_conv_w[j], ssd_conv_b[j], ssd_dt_bias[j], ssd_a_log[j],
                          ssd_d[j], ssd_norm_w[j], ssd_w_out[j])
        else:
            m = mla_mixer(h, mla_w_down[j], mla_q_norm[j], mla_w_uq[j], mla_kv_norm[j], mla_w_ukv[j], mla_w_out[j])
        h = layer_norm(DEEPNORM_ALPHA * h + m, ln_mix_g[i], ln_mix_b[i])
        f = grouped_moe(h, router_w, router_bias, moe_w_gate[i], moe_w_up[i], moe_w_down[i])
        h = layer_norm(DEEPNORM_ALPHA * h + f, ln_ffn_g[i], ln_ffn_b[i])
    return h
```

```python
import functools
import math

import jax
import jax.numpy as jnp
from jax import lax
from jax.experimental import pallas as pl
from jax.experimental.pallas import tpu as pltpu

F32 = jnp.float32
BF16 = jnp.bfloat16
I32 = jnp.int32

D_MODEL = 1024
DEPTH = 2
N_MIXERS = 2

SSD_D_INNER = 2048
SSD_HEAD_DIM = 64
SSD_N_HEADS = 32
SSD_N_GROUPS = 8
SSD_HEADS_PER_GROUP = 4
SSD_D_STATE = 128
SSD_CONV_W = 4
SSD_BC_DIM = SSD_N_GROUPS * SSD_D_STATE
SSD_CONV_DIM = SSD_D_INNER + 2 * SSD_BC_DIM

MLA_N_HEADS = 16
MLA_Q_RANK = 384
MLA_KV_RANK = 256
MLA_NOPE = 64
MLA_ROPE = 32
MLA_V = 64
ROPE_THETA = 10000.0
ATTN_CHUNK = 64

N_EXPERTS = 16
N_EXPERT_GROUPS = 4
EXPERTS_PER_GROUP = 4
D_FF_EXPERT = 512
N_PAIRS = 6
N_BUCKETS = N_EXPERT_GROUPS * N_PAIRS
BUCKET_ROWS = 32

DEEPNORM_ALPHA = (2.0 * DEPTH) ** 0.25
LN_EPS = 1e-5
RMS_EPS = 1e-6

LANES = 128
ROW_SUB = D_MODEL // LANES
HEAD_SLOT = 128
SSD_L = 128
VMEM_LIMIT = 56 * 1024 * 1024

ROW_TILE = 512
MOE_TILE = 256


def _silu(v):
    return v / (1.0 + jnp.exp(-v))


def _softplus(v):
    return jnp.maximum(v, 0.0) + jnp.log1p(jnp.exp(-jnp.abs(v)))


def _dot(a, b):
    return jnp.dot(a, b, preferred_element_type=F32)


def _dot_nt(a, b):
    return lax.dot_general(a, b, (((1,), (1,)), ((), ())), preferred_element_type=F32)


def _dot_tn(a, b):
    return lax.dot_general(a, b, (((0,), (0,)), ((), ())), preferred_element_type=F32)


def _split_bf16(v, terms):
    parts = []
    rem = v
    for _ in range(terms):
        p = rem.astype(BF16)
        parts.append(p)
        rem = rem - p.astype(F32)
    return parts


def _resident(shape):
    nd = len(shape)
    return pl.BlockSpec(shape, lambda *_: (0,) * nd)


def _inproj_kernel(x_ref, wz_ref, wx_ref, wdt_ref, cw_ref, cb_ref, dtb_ref,
                   z_ref, xbc_ref, dt_ref, buf_ref, carry_ref, *, tm, cn):
    @pl.when(pl.program_id(1) == 0)
    def _():
        carry_ref[...] = jnp.zeros_like(carry_ref)

    xb = x_ref[...].astype(BF16)
    for c in range(SSD_D_INNER // cn):
        sl = slice(c * cn, (c + 1) * cn)
        z_ref[:, sl] = _dot(xb, wz_ref[:, sl]).astype(BF16)
    dt_ref[...] = _softplus(_dot(xb, wdt_ref[...]) + dtb_ref[...])
    for c in range(SSD_CONV_DIM // cn):
        sl = slice(c * cn, (c + 1) * cn)
        r = _dot(xb, wx_ref[:, sl])
        buf_ref[0:8, :] = carry_ref[:, sl]
        buf_ref[8:8 + tm, :] = r
        carry_ref[:, sl] = r[tm - 8:tm, :]
        w = cw_ref[:, sl]
        y = (cb_ref[:, sl]
             + w[3:4, :] * r
             + w[2:3, :] * buf_ref[7:7 + tm, :]
             + w[1:2, :] * buf_ref[6:6 + tm, :]
             + w[0:1, :] * buf_ref[5:5 + tm, :])
        xbc_ref[:, sl] = _silu(y).astype(BF16)


def _ssd_inproj(x2, wz, wx, wdt, conv_w, conv_b, dt_bias, *, bsz, seq):
    n = bsz * seq
    tm = min(ROW_TILE, seq)
    cn = 512
    nt = seq // tm
    row = lambda b, j: (b * nt + j, 0)
    kern = functools.partial(_inproj_kernel, tm=tm, cn=cn)
    return pl.pallas_call(
        kern,
        grid=(bsz, nt),
        in_specs=[
            pl.BlockSpec((tm, D_MODEL), row),
            _resident(wz.shape), _resident(wx.shape), _resident(wdt.shape),
            _resident(conv_w.shape), _resident(conv_b.shape), _resident(dt_bias.shape),
        ],
        out_specs=[
            pl.BlockSpec((tm, SSD_D_INNER), row),
            pl.BlockSpec((tm, SSD_CONV_DIM), row),
            pl.BlockSpec((tm, LANES), row),
        ],
        out_shape=[
            jax.ShapeDtypeStruct((n, SSD_D_INNER), BF16),
            jax.ShapeDtypeStruct((n, SSD_CONV_DIM), BF16),
            jax.ShapeDtypeStruct((n, LANES), F32),
        ],
        scratch_shapes=[pltpu.VMEM((8 + tm, cn), F32), pltpu.VMEM((8, SSD_CONV_DIM), F32)],
        compiler_params=pltpu.CompilerParams(
            dimension_semantics=("arbitrary", "arbitrary"), vmem_limit_bytes=VMEM_LIMIT),
        name="ssd_inproj_conv",
    )(x2, wz, wx, wdt, conv_w, conv_b, dt_bias)


def _ssd_scan_kernel(xbc_ref, z_ref, dt_ref, arow_ref, dexp_ref, nw_ref, e128_ref, e64_ref,
                     y_ref, state_ref):
    L = SSD_L
    G, R, P, NS = SSD_N_GROUPS, SSD_HEADS_PER_GROUP, SSD_HEAD_DIM, SSD_D_STATE

    @pl.when(pl.program_id(1) == 0)
    def _():
        state_ref[...] = jnp.zeros_like(state_ref)

    row_i = lax.broadcasted_iota(I32, (L, L), 0)
    col_i = lax.broadcasted_iota(I32, (L, L), 1)
    tril = jnp.where(col_i <= row_i, 1.0, 0.0).astype(BF16)
    eye = jnp.where(col_i == row_i, 1.0, 0.0)
    neg = jnp.where(col_i <= row_i, 0.0, -jnp.inf)

    dt = dt_ref[...]
    a = dt * arow_ref[...]
    a_parts = _split_bf16(a, 3)
    acum = _dot(tril, a_parts[0]) + _dot(tril, a_parts[1]) + _dot(tril, a_parts[2])

    ac_parts = _split_bf16(acum, 3)
    dt_parts = _split_bf16(dt, 2)
    lhs_t = jnp.concatenate(ac_parts, axis=0)
    ex_t = _dot(lhs_t, e128_ref[...])
    acum_t = ex_t[0:L] + ex_t[L:2 * L] + ex_t[2 * L:3 * L]
    lhs_p = jnp.concatenate(ac_parts + dt_parts, axis=0)
    ex_p = _dot(lhs_p, e64_ref[...])
    acum_p = ex_p[0:L] + ex_p[L:2 * L] + ex_p[2 * L:3 * L]
    dt_p = ex_p[3 * L:4 * L] + ex_p[4 * L:5 * L]

    x = xbc_ref[:, 0:SSD_D_INNER].astype(F32)
    last_p = acum_p[L - 1:L, :]
    xdt = (x * dt_p).astype(BF16)
    xw = (x * (jnp.exp(last_p - acum_p) * dt_p)).astype(BF16)
    exp_a_p = jnp.exp(acum_p)
    exp_last_p = jnp.exp(last_p)

    lane = lax.broadcasted_iota(I32, (L, LANES), 1)
    lo_half = lane < P

    y_parts = []
    for g in range(G):
        b_g = xbc_ref[:, SSD_D_INNER + g * NS:SSD_D_INNER + (g + 1) * NS]
        c_g = xbc_ref[:, SSD_D_INNER + SSD_BC_DIM + g * NS:SSD_D_INNER + SSD_BC_DIM + (g + 1) * NS]
        scores = _dot_nt(c_g, b_g)
        gsl = slice(g * R * P, (g + 1) * R * P)
        y_off = _dot(c_g, state_ref[g].astype(BF16)) * exp_a_p[:, gsl]
        y_diag = []
        for pr in range(R // 2):
            h0 = g * R + 2 * pr
            m_pair = []
            for h in (h0, h0 + 1):
                blk = acum_t[:, h * L:(h + 1) * L]
                rowv = jnp.sum(blk * eye, axis=0, keepdims=True)
                m_pair.append((scores * jnp.exp(blk - rowv + neg)).astype(BF16))
            xs = xdt[:, h0 * P:(h0 + 2) * P]
            zero = jnp.zeros_like(xs)
            bd = jnp.concatenate([jnp.where(lo_half, xs, zero), jnp.where(lo_half, zero, xs)], axis=0)
            y_diag.append(_dot(jnp.concatenate(m_pair, axis=1), bd))
        y_parts.append(jnp.concatenate(y_diag, axis=1) + y_off)
        state_ref[g] = state_ref[g] * exp_last_p[:, gsl] + _dot_tn(b_g, xw[:, gsl])

    y = jnp.concatenate(y_parts, axis=1) + x * dexp_ref[...]
    yz = y * _silu(z_ref[...].astype(F32))
    outs = []
    for g in range(G):
        blk = yz[:, g * R * P:(g + 1) * R * P]
        ms = jnp.mean(blk * blk, axis=-1, keepdims=True)
        outs.append(blk * lax.rsqrt(ms + RMS_EPS))
    y_ref[...] = (jnp.concatenate(outs, axis=1) * nw_ref[...]).astype(BF16)


def _ssd_scan(xbc, z, dt, a_row, d_exp, norm_w, e128, e64, *, bsz, seq):
    n = bsz * seq
    L = SSD_L
    nt = seq // L
    row = lambda b, j: (b * nt + j, 0)
    return pl.pallas_call(
        _ssd_scan_kernel,
        grid=(bsz, nt),
        in_specs=[
            pl.BlockSpec((L, SSD_CONV_DIM), row),
            pl.BlockSpec((L, SSD_D_INNER), row),
            pl.BlockSpec((L, LANES), row),
            _resident(a_row.shape), _resident(d_exp.shape), _resident(norm_w.shape),
            _resident(e128.shape), _resident(e64.shape),
        ],
        out_specs=pl.BlockSpec((L, SSD_D_INNER), row),
        out_shape=jax.ShapeDtypeStruct((n, SSD_D_INNER), BF16),
        scratch_shapes=[pltpu.VMEM((SSD_N_GROUPS, SSD_D_STATE, SSD_HEADS_PER_GROUP * SSD_HEAD_DIM), F32)],
        compiler_params=pltpu.CompilerParams(
            dimension_semantics=("arbitrary", "arbitrary"), vmem_limit_bytes=VMEM_LIMIT),
        name="ssd_scan_gate_norm",
    )(xbc, z, dt, a_row, d_exp, norm_w, e128, e64)


def _layer_norm(v, g, b):
    mu = jnp.mean(v, axis=-1, keepdims=True)
    c = v - mu
    var = jnp.mean(c * c, axis=-1, keepdims=True)
    return c * lax.rsqrt(var + LN_EPS) * g + b


def _proj_ln_kernel(a_ref, w_ref, res_ref, g_ref, b_ref, o_ref):
    m = _dot(a_ref[...], w_ref[...])
    o_ref[...] = _layer_norm(DEEPNORM_ALPHA * res_ref[...] + m, g_ref[...], b_ref[...])


def _proj_ln(a, w, res, g, b):
    n, k = a.shape
    tm = min(ROW_TILE, n)
    row = lambda i: (i, 0)
    return pl.pallas_call(
        _proj_ln_kernel,
        grid=(n // tm,),
        in_specs=[pl.BlockSpec((tm, k), row), _resident(w.shape), pl.BlockSpec((tm, D_MODEL), row),
                  _resident(g.shape), _resident(b.shape)],
        out_specs=pl.BlockSpec((tm, D_MODEL), row),
        out_shape=jax.ShapeDtypeStruct((n, D_MODEL), F32),
        compiler_params=pltpu.CompilerParams(
            dimension_semantics=("arbitrary",), vmem_limit_bytes=VMEM_LIMIT),
        name="outproj_deepnorm_ln",
    )(a, w, res, g, b)


_PAIRS = ((0, 1), (0, 2), (0, 3), (1, 2), (1, 3), (2, 3))


def _router_kernel(h_ref, rwt_ref, rb_ref, bucket_ref, rank_ref, counts_ref, base_ref, *, tm):
    @pl.when(pl.program_id(0) == 0)
    def _():
        base_ref[...] = jnp.zeros_like(base_ref)

    logits = lax.dot_general(rwt_ref[...], h_ref[...], (((1,), (1,)), ((), ())),
                             preferred_element_type=F32, precision=lax.Precision.HIGHEST)
    sel = 1.0 / (1.0 + jnp.exp(-logits)) + rb_ref[...]
    rows = [sel[e:e + 1, :] for e in range(N_EXPERTS)]

    best_g = jnp.zeros((1, tm), I32)
    best_s = None
    for g in range(N_EXPERT_GROUPS):
        v = rows[4 * g:4 * g + 4]
        gs = None
        for (i, j) in _PAIRS:
            s = v[i] + v[j]
            gs = s if gs is None else jnp.maximum(gs, s)
        if best_s is None:
            best_s = gs
        else:
            better = gs > best_s
            best_g = jnp.where(better, g, best_g)
            best_s = jnp.where(better, gs, best_s)

    v = []
    for i in range(EXPERTS_PER_GROUP):
        acc = rows[i]
        for g in range(1, N_EXPERT_GROUPS):
            acc = jnp.where(best_g == g, rows[4 * g + i], acc)
        v.append(acc)
    keep = []
    for i in range(EXPERTS_PER_GROUP):
        beaten = jnp.zeros((1, tm), I32)
        for j in range(EXPERTS_PER_GROUP):
            if j == i:
                continue
            wins = (v[j] > v[i]) if j > i else (v[j] >= v[i])
            beaten = beaten + jnp.where(wins, 1, 0)
        keep.append(jnp.where(beaten < 2, 1, 0))
    pair = jnp.full((1, tm), N_PAIRS - 1, I32)
    for p in range(N_PAIRS - 2, -1, -1):
        i, j = _PAIRS[p]
        pair = jnp.where(keep[i] * keep[j] > 0, p, pair)
    bucket = best_g * N_PAIRS + pair

    brow = lax.broadcasted_iota(I32, (BUCKET_ROWS, tm), 0)
    onehot = jnp.where(brow == bucket, 1.0, 0.0)
    ti = lax.broadcasted_iota(I32, (tm, tm), 0)
    tj = lax.broadcasted_iota(I32, (tm, tm), 1)
    before = jnp.where(ti < tj, 1.0, 0.0).astype(BF16)
    prefix = _dot(onehot.astype(BF16), before)
    base = base_ref[:, 0:1]
    rank = jnp.sum(onehot * (prefix + base), axis=0, keepdims=True)
    cnt = jnp.sum(onehot, axis=1, keepdims=True)
    base_ref[...] = base_ref[...] + cnt
    bucket_ref[...] = bucket
    rank_ref[...] = rank.astype(I32)
    counts_ref[...] = base_ref[...].astype(I32)


def _router(h, rw_t, rb_col):
    n = h.shape[0]
    tm = min(ROW_TILE, n)
    nt = n // tm
    kern = functools.partial(_router_kernel, tm=tm)
    bucket, rank, counts = pl.pallas_call(
        kern,
        grid=(nt,),
        in_specs=[pl.BlockSpec((tm, D_MODEL), lambda i: (i, 0)), _resident(rw_t.shape), _resident(rb_col.shape)],
        out_specs=[pl.BlockSpec((None, 1, tm), lambda i: (i, 0, 0)),
                   pl.BlockSpec((None, 1, tm), lambda i: (i, 0, 0)),
                   pl.BlockSpec((BUCKET_ROWS, LANES), lambda i: (0, 0))],
        out_shape=[jax.ShapeDtypeStruct((nt, 1, tm), I32), jax.ShapeDtypeStruct((nt, 1, tm), I32),
                   jax.ShapeDtypeStruct((BUCKET_ROWS, LANES), I32)],
        scratch_shapes=[pltpu.VMEM((BUCKET_ROWS, LANES), F32)],
        compiler_params=pltpu.CompilerParams(
            dimension_semantics=("arbitrary",), vmem_limit_bytes=VMEM_LIMIT),
        name="moe_router",
    )(h, rw_t, rb_col)
    return bucket.reshape(n), rank.reshape(n), counts[:N_BUCKETS, 0]


def _rows_to_tiles(ref, v, rows):
    for j in range(ROW_SUB):
        ref[pl.ds(j, rows, stride=ROW_SUB), :] = v[:, j * LANES:(j + 1) * LANES]


def _tiles_to_rows(ref, rows):
    return jnp.concatenate([ref[pl.ds(j, rows, stride=ROW_SUB), :] for j in range(ROW_SUB)], axis=1)


def _row_tile(ref, r):
    return ref.at[pl.ds(pl.multiple_of(r * ROW_SUB, ROW_SUB), ROW_SUB)]


def _scatter_rows_kernel(zstart_ref, zvalid_ref, pos_ref, h_ref, o_ref, buf_ref, zeros_ref, sem, zsem, *, tm):
    def zero_copy(b):
        start = pl.multiple_of(zstart_ref[b] * ROW_SUB, ROW_SUB)
        return pltpu.make_async_copy(zeros_ref, o_ref.at[pl.ds(start, MOE_TILE * ROW_SUB)], zsem)

    @pl.when(pl.program_id(0) == 0)
    def _():
        zeros_ref[...] = jnp.zeros_like(zeros_ref)
        for b in range(2 * N_BUCKETS):
            pl.when(zvalid_ref[b] > 0)(lambda b=b: zero_copy(b).start())
        for b in range(2 * N_BUCKETS):
            pl.when(zvalid_ref[b] > 0)(lambda b=b: zero_copy(b).wait())

    _rows_to_tiles(buf_ref, h_ref[...], tm)

    def row_copy(r):
        return pltpu.make_async_copy(_row_tile(buf_ref, r), _row_tile(o_ref, pos_ref[0, r]), sem)

    def issue(r, carry):
        row_copy(r).start()
        return carry

    def drain(r, carry):
        row_copy(r).wait()
        return carry

    lax.fori_loop(0, tm, issue, 0)
    lax.fori_loop(0, tm, drain, 0)


def _scatter_rows(h, pos, zstart, zvalid, n_sorted):
    n = h.shape[0]
    tm = min(ROW_TILE, n)
    nt = n // tm
    kern = functools.partial(_scatter_rows_kernel, tm=tm)
    grid_spec = pltpu.PrefetchScalarGridSpec(
        num_scalar_prefetch=2,
        grid=(nt,),
        in_specs=[pl.BlockSpec((None, 1, tm), lambda i, zs, zv: (i, 0, 0), memory_space=pltpu.SMEM),
                  pl.BlockSpec((tm, D_MODEL), lambda i, zs, zv: (i, 0))],
        out_specs=pl.BlockSpec(memory_space=pl.ANY),
        scratch_shapes=[pltpu.VMEM((tm * ROW_SUB, LANES), F32), pltpu.VMEM((MOE_TILE * ROW_SUB, LANES), F32),
                        pltpu.SemaphoreType.DMA, pltpu.SemaphoreType.DMA],
    )
    return pl.pallas_call(
        kern,
        grid_spec=grid_spec,
        out_shape=jax.ShapeDtypeStruct((n_sorted * ROW_SUB, LANES), F32),
        compiler_params=pltpu.CompilerParams(
            dimension_semantics=("arbitrary",), vmem_limit_bytes=VMEM_LIMIT),
        name="moe_scatter_rows",
    )(zstart, zvalid, pos.reshape(nt, 1, tm), h)


def _expert_kernel(blk_ref, ea_ref, eb_ref, nvalid_ref,
                   x_ref, rwa_ref, rwb_ref, wga_ref, wua_ref, wda_ref, wgb_ref, wub_ref, wdb_ref, o_ref):
    @pl.when(pl.program_id(0) < nvalid_ref[0])
    def _():
        x = _tiles_to_rows(x_ref, MOE_TILE)
        xb = x.astype(BF16)
        aff_a = 1.0 / (1.0 + jnp.exp(-jnp.sum(x * rwa_ref[...], axis=-1, keepdims=True)))
        aff_b = 1.0 / (1.0 + jnp.exp(-jnp.sum(x * rwb_ref[...], axis=-1, keepdims=True)))
        tot = aff_a + aff_b
        h_a = (_silu(_dot(xb, wga_ref[...])) * _dot(xb, wua_ref[...]) * (aff_a / tot)).astype(BF16)
        h_b = (_silu(_dot(xb, wgb_ref[...])) * _dot(xb, wub_ref[...]) * (aff_b / tot)).astype(BF16)
        _rows_to_tiles(o_ref, _dot(h_a, wda_ref[...]) + _dot(h_b, wdb_ref[...]), MOE_TILE)

    @pl.when(pl.program_id(0) >= nvalid_ref[0])
    def _():
        o_ref[...] = jnp.zeros_like(o_ref)


def _expert_mlp(x_sorted, tile_blk, tile_ea, tile_eb, nvalid, rw_rows, wg, wu, wd):
    n_sorted = x_sorted.shape[0] // ROW_SUB
    nt = n_sorted // MOE_TILE
    xmap = lambda i, blk, ea, eb, nv: (blk[i], 0)
    amap = lambda i, blk, ea, eb, nv: (ea[i], 0, 0)
    bmap = lambda i, blk, ea, eb, nv: (eb[i], 0, 0)
    w_in = (None, D_MODEL, D_FF_EXPERT)
    w_out = (None, D_FF_EXPERT, D_MODEL)
    grid_spec = pltpu.PrefetchScalarGridSpec(
        num_scalar_prefetch=4,
        grid=(nt,),
        in_specs=[pl.BlockSpec((MOE_TILE * ROW_SUB, LANES), xmap),
                  pl.BlockSpec((None, 1, D_MODEL), amap), pl.BlockSpec((None, 1, D_MODEL), bmap),
                  pl.BlockSpec(w_in, amap), pl.BlockSpec(w_in, amap), pl.BlockSpec(w_out, amap),
                  pl.BlockSpec(w_in, bmap), pl.BlockSpec(w_in, bmap), pl.BlockSpec(w_out, bmap)],
        out_specs=pl.BlockSpec((MOE_TILE * ROW_SUB, LANES), lambda i, blk, ea, eb, nv: (i, 0)),
    )
    return pl.pallas_call(
        _expert_kernel,
        grid_spec=grid_spec,
        out_shape=jax.ShapeDtypeStruct((n_sorted * ROW_SUB, LANES), F32),
        compiler_params=pltpu.CompilerParams(
            dimension_semantics=("arbitrary",), vmem_limit_bytes=VMEM_LIMIT),
        name="moe_expert_mlp",
    )(tile_blk, tile_ea, tile_eb, nvalid, x_sorted, rw_rows, rw_rows, wg, wu, wd, wg, wu, wd)


def _gather_ln_kernel(pos_ref, f_ref, h_ref, g_ref, b_ref, o_ref, buf_ref, sem, *, tm):
    def row_copy(r):
        return pltpu.make_async_copy(_row_tile(f_ref, pos_ref[0, r]), _row_tile(buf_ref, r), sem)

    def issue(r, carry):
        row_copy(r).start()
        return carry

    def drain(r, carry):
        row_copy(r).wait()
        return carry

    lax.fori_loop(0, tm, issue, 0)
    lax.fori_loop(0, tm, drain, 0)
    f = _tiles_to_rows(buf_ref, tm)
    o_ref[...] = _layer_norm(DEEPNORM_ALPHA * h_ref[...] + f, g_ref[...], b_ref[...])


def _gather_ln(f_sorted, pos, h, g, b):
    n = h.shape[0]
    tm = min(ROW_TILE, n)
    nt = n // tm
    kern = functools.partial(_gather_ln_kernel, tm=tm)
    return pl.pallas_call(
        kern,
        grid=(nt,),
        in_specs=[pl.BlockSpec((None, 1, tm), lambda i: (i, 0, 0), memory_space=pltpu.SMEM),
                  pl.BlockSpec(memory_space=pl.ANY),
                  pl.BlockSpec((tm, D_MODEL), lambda i: (i, 0)),
                  _resident(g.shape), _resident(b.shape)],
        out_specs=pl.BlockSpec((tm, D_MODEL), lambda i: (i, 0)),
        out_shape=jax.ShapeDtypeStruct((n, D_MODEL), F32),
        scratch_shapes=[pltpu.VMEM((tm * ROW_SUB, LANES), F32), pltpu.SemaphoreType.DMA],
        compiler_params=pltpu.CompilerParams(
            dimension_semantics=("arbitrary",), vmem_limit_bytes=VMEM_LIMIT),
        name="moe_gather_deepnorm_ln",
    )(pos.reshape(nt, 1, tm), f_sorted, h, g, b)


def _moe_layer(h, rw_t, rb_col, rw_rows, wg, wu, wd, ln_g, ln_b):
    n = h.shape[0]
    n_tiles = n // MOE_TILE + N_BUCKETS
    n_sorted = n_tiles * MOE_TILE
    bucket, rank, counts = _router(h, rw_t, rb_col)
    tiles_per = (counts + MOE_TILE - 1) // MOE_TILE
    tile_end = jnp.cumsum(tiles_per)
    tile_start = tile_end - tiles_per
    pos = tile_start[bucket] * MOE_TILE + rank
    nvalid = tile_end[-1]
    tail = nvalid + jnp.arange(N_BUCKETS, dtype=I32)
    zstart = jnp.concatenate([jnp.maximum(tile_end - 1, 0), jnp.minimum(tail, n_tiles - 1)]) * MOE_TILE
    zvalid = jnp.concatenate([tiles_per > 0, tail < n_tiles]).astype(I32)
    tidx = jnp.minimum(jnp.arange(n_tiles, dtype=I32), nvalid - 1)
    tb = jnp.searchsorted(tile_end, tidx, side="right").astype(I32)
    pair_lo = jnp.array([p[0] for p in _PAIRS], I32)
    pair_hi = jnp.array([p[1] for p in _PAIRS], I32)
    tile_ea = (tb // N_PAIRS) * EXPERTS_PER_GROUP + pair_lo[tb % N_PAIRS]
    tile_eb = (tb // N_PAIRS) * EXPERTS_PER_GROUP + pair_hi[tb % N_PAIRS]
    x_sorted = _scatter_rows(h, pos.astype(I32), zstart.astype(I32), zvalid, n_sorted)
    f_sorted = _expert_mlp(x_sorted, tidx, tile_ea.astype(I32), tile_eb.astype(I32),
                           nvalid.reshape(1).astype(I32), rw_rows, wg, wu, wd)
    return _gather_ln(f_sorted, pos.astype(I32), h, ln_g, ln_b)


def _rot_half_slot(v, lane):
    w = v.shape[1]
    fwd = pltpu.roll(v, MLA_ROPE // 2, 1)
    bwd = pltpu.roll(v, w - MLA_ROPE // 2, 1)
    first = (lane >= MLA_NOPE) & (lane < MLA_NOPE + MLA_ROPE // 2)
    second = (lane >= MLA_NOPE + MLA_ROPE // 2) & (lane < MLA_NOPE + MLA_ROPE)
    return jnp.where(first, -bwd, jnp.where(second, fwd, 0.0))


def _mla_proj_kernel(x_ref, wd_ref, wkr_ref, qn_ref, kvn_ref, wuq_ref, wuk_ref, wuv_ref, cos_ref, sin_ref,
                     q_ref, k_ref, v_ref, *, tm):
    xb = x_ref[...].astype(BF16)
    down = _dot(xb, wd_ref[...])
    c_q = down[:, 0:MLA_Q_RANK]
    c_kv = down[:, MLA_Q_RANK:MLA_Q_RANK + MLA_KV_RANK]
    c_q = c_q * lax.rsqrt(jnp.mean(c_q * c_q, axis=-1, keepdims=True) + RMS_EPS) * qn_ref[...]
    c_kv = c_kv * lax.rsqrt(jnp.mean(c_kv * c_kv, axis=-1, keepdims=True) + RMS_EPS) * kvn_ref[...]
    c_q = c_q.astype(BF16)
    c_kv = c_kv.astype(BF16)

    cos = cos_ref[...]
    sin = sin_ref[...]
    lane = lax.broadcasted_iota(I32, (tm, HEAD_SLOT), 1)
    kr = _dot(xb, wkr_ref[...])
    kr = kr * cos + _rot_half_slot(kr, lane) * sin
    scale = 1.0 / math.sqrt(MLA_NOPE + MLA_ROPE)
    for h in range(MLA_N_HEADS):
        sl = slice(h * HEAD_SLOT, (h + 1) * HEAD_SLOT)
        q = _dot(c_q, wuq_ref[:, sl])
        q = (q * cos + _rot_half_slot(q, lane) * sin) * scale
        q_ref[:, sl] = q.astype(BF16)
        k_ref[:, sl] = (_dot(c_kv, wuk_ref[:, sl]) + kr).astype(BF16)
        v_ref[:, sl] = _dot(c_kv, wuv_ref[:, sl]).astype(BF16)


def _mla_proj(h2, wd, wkr, qn, kvn, wuq, wuk, wuv, cos_t, sin_t, *, bsz, seq):
    n = bsz * seq
    tm = min(ROW_TILE, seq)
    nt = seq // tm
    row = lambda b, j: (b * nt + j, 0)
    tab = lambda b, j: (j, 0)
    wide = MLA_N_HEADS * HEAD_SLOT
    kern = functools.partial(_mla_proj_kernel, tm=tm)
    return pl.pallas_call(
        kern,
        grid=(bsz, nt),
        in_specs=[pl.BlockSpec((tm, D_MODEL), row), _resident(wd.shape), _resident(wkr.shape),
                  _resident(qn.shape), _resident(kvn.shape), _resident(wuq.shape), _resident(wuk.shape),
                  _resident(wuv.shape), pl.BlockSpec((tm, HEAD_SLOT), tab), pl.BlockSpec((tm, HEAD_SLOT), tab)],
        out_specs=[pl.BlockSpec((tm, wide), row)] * 3,
        out_shape=[jax.ShapeDtypeStruct((n, wide), BF16)] * 3,
        compiler_params=pltpu.CompilerParams(
            dimension_semantics=("arbitrary", "arbitrary"), vmem_limit_bytes=VMEM_LIMIT),
        name="mla_proj_rope",
    )(h2, wd, wkr, qn, kvn, wuq, wuk, wuv, cos_t, sin_t)


def _attn_kernel(q_ref, k_ref, v_ref, o_ref, *, tq):
    qi = pl.program_id(2)
    q = q_ref[...]

    def step(s, m, l, acc, k0):
        m_new = jnp.maximum(m, jnp.max(s, axis=-1, keepdims=True))
        p = jnp.exp(s - m_new)
        corr = jnp.exp(m - m_new)
        l_new = corr * l + jnp.sum(p, axis=-1, keepdims=True)
        acc_new = corr * acc + _dot(p.astype(BF16), v_ref[pl.ds(k0, tq), :])
        return m_new, l_new, acc_new

    def body(j, carry):
        m, l, acc = carry
        k0 = pl.multiple_of(j * tq, tq)
        s = _dot_nt(q, k_ref[pl.ds(k0, tq), :])
        return step(s, m, l, acc, k0)

    init = (jnp.full((tq, 1), -jnp.inf, F32), jnp.zeros((tq, 1), F32), jnp.zeros((tq, HEAD_SLOT), F32))
    m, l, acc = lax.fori_loop(0, qi, body, init)
    k0 = pl.multiple_of(qi * tq, tq)
    s = _dot_nt(q, k_ref[pl.ds(k0, tq), :])
    qc = lax.broadcasted_iota(I32, (tq, tq), 0) // ATTN_CHUNK
    kc = lax.broadcasted_iota(I32, (tq, tq), 1) // ATTN_CHUNK
    s = jnp.where(kc <= qc, s, -jnp.inf)
    m, l, acc = step(s, m, l, acc, k0)
    o_ref[...] = (acc / l).astype(BF16)


def _attention(q, k, v, *, bsz, seq):
    n = bsz * seq
    tq = min(ROW_TILE, seq)
    nq = seq // tq
    kern = functools.partial(_attn_kernel, tq=tq)
    return pl.pallas_call(
        kern,
        grid=(bsz, MLA_N_HEADS, nq),
        in_specs=[pl.BlockSpec((tq, HEAD_SLOT), lambda b, h, i: (b * nq + i, h)),
                  pl.BlockSpec((seq, HEAD_SLOT), lambda b, h, i: (b, h)),
                  pl.BlockSpec((seq, HEAD_SLOT), lambda b, h, i: (b, h))],
        out_specs=pl.BlockSpec((tq, HEAD_SLOT), lambda b, h, i: (b * nq + i, h)),
        out_shape=jax.ShapeDtypeStruct((n, MLA_N_HEADS * HEAD_SLOT), BF16),
        compiler_params=pltpu.CompilerParams(
            dimension_semantics=("arbitrary", "arbitrary", "arbitrary"), vmem_limit_bytes=VMEM_LIMIT),
        name="mla_flash_attention",
    )(q, k, v)


def _head_slots(w, head_dim, n_heads):
    k = w.shape[0]
    w3 = w.reshape(k, n_heads, head_dim)
    return jnp.pad(w3, ((0, 0), (0, 0), (0, HEAD_SLOT - head_dim))).reshape(k, n_heads * HEAD_SLOT)


def kernel(x, ssd_w_in, ssd_conv_w, ssd_conv_b, ssd_dt_bias, ssd_a_log, ssd_d, ssd_norm_w, ssd_w_out, mla_w_down, mla_q_norm, mla_w_uq, mla_kv_norm, mla_w_ukv, mla_w_out, router_w, router_bias, moe_w_gate, moe_w_up, moe_w_down, ln_mix_g, ln_mix_b, ln_ffn_g, ln_ffn_b):
    bsz, seq, _ = x.shape
    n = bsz * seq
    h = x.reshape(n, D_MODEL)

    rw_t = router_w.T
    rw_rows = rw_t.reshape(N_EXPERTS, 1, D_MODEL)
    rb_col = router_bias.reshape(N_EXPERTS, 1)
    head_of_lane_p = jnp.arange(SSD_D_INNER) // SSD_HEAD_DIM
    e64 = (jnp.arange(LANES)[:, None] == head_of_lane_p[None, :]).astype(BF16)
    e128 = (jnp.arange(LANES)[:, None] == (jnp.arange(SSD_N_HEADS * SSD_L) // SSD_L)[None, :]).astype(BF16)

    for i in range(DEPTH):
        j = i // N_MIXERS
        if i % N_MIXERS == 0:
            w_in = ssd_w_in[j]
            wz = w_in[:, :SSD_D_INNER].astype(BF16)
            wx = w_in[:, SSD_D_INNER:SSD_D_INNER + SSD_CONV_DIM].astype(BF16)
            pad_h = LANES - SSD_N_HEADS
            wdt = jnp.pad(w_in[:, SSD_D_INNER + SSD_CONV_DIM:], ((0, 0), (0, pad_h))).astype(BF16)
            dt_bias = jnp.pad(ssd_dt_bias[j], (0, pad_h)).reshape(1, LANES)
            a_row = jnp.pad(-jnp.exp(ssd_a_log[j]), (0, pad_h)).reshape(1, LANES)
            d_exp = ssd_d[j][head_of_lane_p].reshape(1, SSD_D_INNER)
            z, xbc, dt = _ssd_inproj(h, wz, wx, wdt, ssd_conv_w[j], ssd_conv_b[j].reshape(1, SSD_CONV_DIM),
                                     dt_bias, bsz=bsz, seq=seq)
            y = _ssd_scan(xbc, z, dt, a_row, d_exp, ssd_norm_w[j].reshape(1, SSD_D_INNER), e128, e64,
                          bsz=bsz, seq=seq)
            w_o = ssd_w_out[j].astype(BF16)
        else:
            w_down = mla_w_down[j]
            wd = w_down[:, :MLA_Q_RANK + MLA_KV_RANK].astype(BF16)
            wkr = jnp.pad(w_down[:, MLA_Q_RANK + MLA_KV_RANK:],
                          ((0, 0), (MLA_NOPE, HEAD_SLOT - MLA_NOPE - MLA_ROPE))).astype(BF16)
            wuq = _head_slots(mla_w_uq[j], MLA_NOPE + MLA_ROPE, MLA_N_HEADS).astype(BF16)
            wukv = mla_w_ukv[j].reshape(MLA_KV_RANK, MLA_N_HEADS, MLA_NOPE + MLA_V)
            wuk = _head_slots(wukv[:, :, :MLA_NOPE].reshape(MLA_KV_RANK, -1), MLA_NOPE, MLA_N_HEADS).astype(BF16)
            wuv = _head_slots(wukv[:, :, MLA_NOPE:].reshape(MLA_KV_RANK, -1), MLA_V, MLA_N_HEADS).astype(BF16)
            inv = ROPE_THETA ** (-jnp.arange(0, MLA_ROPE, 2, dtype=F32) / MLA_ROPE)
            ang = jnp.arange(seq, dtype=F32)[:, None] * inv[None, :]
            ones = jnp.ones((seq, MLA_NOPE), F32)
            zeros = jnp.zeros((seq, HEAD_SLOT - MLA_NOPE - MLA_ROPE), F32)
            cos_t = jnp.concatenate([ones, jnp.cos(ang), jnp.cos(ang), zeros], axis=1)
            sin_t = jnp.concatenate([0.0 * ones, jnp.sin(ang), jnp.sin(ang), zeros], axis=1)
            q, k, v = _mla_proj(h, wd, wkr, mla_q_norm[j].reshape(1, MLA_Q_RANK),
                                mla_kv_norm[j].reshape(1, MLA_KV_RANK), wuq, wuk, wuv, cos_t, sin_t,
                                bsz=bsz, seq=seq)
            y = _attention(q, k, v, bsz=bsz, seq=seq)
            w_o3 = mla_w_out[j].reshape(MLA_N_HEADS, MLA_V, D_MODEL)
            w_o = jnp.pad(w_o3, ((0, 0), (0, HEAD_SLOT - MLA_V), (0, 0))).reshape(-1, D_MODEL).astype(BF16)
        h = _proj_ln(y, w_o, h, ln_mix_g[i].reshape(1, D_MODEL), ln_mix_b[i].reshape(1, D_MODEL))
        h = _moe_layer(h, rw_t, rb_col, rw_rows, moe_w_gate[i].astype(BF16), moe_w_up[i].astype(BF16),
                       moe_w_down[i].astype(BF16), ln_ffn_g[i].reshape(1, D_MODEL), ln_ffn_b[i].reshape(1, D_MODEL))
    return h.reshape(bsz, seq, D_MODEL)
```

```python
import functools
import math

import jax
import jax.numpy as jnp
from jax import lax
from jax.experimental import pallas as pl
from jax.experimental.pallas import tpu as pltpu

F32 = jnp.float32
BF16 = jnp.bfloat16
I32 = jnp.int32

D_MODEL = 1024
DEPTH = 2
N_MIXERS = 2

SSD_D_INNER = 2048
SSD_HEAD_DIM = 64
SSD_N_HEADS = 32
SSD_N_GROUPS = 8
SSD_HEADS_PER_GROUP = 4
SSD_D_STATE = 128
SSD_CONV_W = 4
SSD_BC_DIM = SSD_N_GROUPS * SSD_D_STATE
SSD_CONV_DIM = SSD_D_INNER + 2 * SSD_BC_DIM

MLA_N_HEADS = 16
MLA_Q_RANK = 384
MLA_KV_RANK = 256
MLA_NOPE = 64
MLA_ROPE = 32
MLA_V = 64
ROPE_THETA = 10000.0
ATTN_CHUNK = 64

N_EXPERTS = 16
N_EXPERT_GROUPS = 4
EXPERTS_PER_GROUP = 4
D_FF_EXPERT = 512
N_PAIRS = 6
N_BUCKETS = N_EXPERT_GROUPS * N_PAIRS
BUCKET_ROWS = 32

DEEPNORM_ALPHA = (2.0 * DEPTH) ** 0.25
LN_EPS = 1e-5
RMS_EPS = 1e-6

LANES = 128
ROW_SUB = D_MODEL // LANES
HEAD_SLOT = 128
SSD_L = 128
VMEM_LIMIT = 56 * 1024 * 1024

ROW_TILE = 512
MOE_TILE = 256


def _silu(v):
    return v / (1.0 + jnp.exp(-v))


def _softplus(v):
    return jnp.maximum(v, 0.0) + jnp.log1p(jnp.exp(-jnp.abs(v)))


def _dot(a, b):
    return jnp.dot(a, b, preferred_element_type=F32)


def _dot_nt(a, b):
    return lax.dot_general(a, b, (((1,), (1,)), ((), ())), preferred_element_type=F32)


def _dot_tn(a, b):
    return lax.dot_general(a, b, (((0,), (0,)), ((), ())), preferred_element_type=F32)


def _split_bf16(v, terms):
    parts = []
    rem = v
    for _ in range(terms):
        p = rem.astype(BF16)
        parts.append(p)
        rem = rem - p.astype(F32)
    return parts


def _resident(shape):
    nd = len(shape)
    return pl.BlockSpec(shape, lambda *_: (0,) * nd)


def _inproj_kernel(x_ref, wz_ref, wx_ref, wdt_ref, cw_ref, cb_ref, dtb_ref,
                   z_ref, xbc_ref, dt_ref, buf_ref, carry_ref, *, tm, cn):
    @pl.when(pl.program_id(1) == 0)
    def _():
        carry_ref[...] = jnp.zeros_like(carry_ref)

    xb = x_ref[...].astype(BF16)
    for c in range(SSD_D_INNER // cn):
        sl = slice(c * cn, (c + 1) * cn)
        z_ref[:, sl] = _dot(xb, wz_ref[:, sl]).astype(BF16)
    dt_ref[...] = _softplus(_dot(xb, wdt_ref[...]) + dtb_ref[...])
    for c in range(SSD_CONV_DIM // cn):
        sl = slice(c * cn, (c + 1) * cn)
        r = _dot(xb, wx_ref[:, sl])
        buf_ref[0:8, :] = carry_ref[:, sl]
        buf_ref[8:8 + tm, :] = r
        carry_ref[:, sl] = r[tm - 8:tm, :]
        w = cw_ref[:, sl]
        y = (cb_ref[:, sl]
             + w[3:4, :] * r
             + w[2:3, :] * buf_ref[7:7 + tm, :]
             + w[1:2, :] * buf_ref[6:6 + tm, :]
             + w[0:1, :] * buf_ref[5:5 + tm, :])
        xbc_ref[:, sl] = _silu(y).astype(BF16)


def _ssd_inproj(x2, wz, wx, wdt, conv_w, conv_b, dt_bias, *, bsz, seq):
    n = bsz * seq
    tm = min(ROW_TILE, seq)
    cn = 512
    nt = seq // tm
    row = lambda b, j: (b * nt + j, 0)
    kern = functools.partial(_inproj_kernel, tm=tm, cn=cn)
    return pl.pallas_call(
        kern,
        grid=(bsz, nt),
        in_specs=[
            pl.BlockSpec((tm, D_MODEL), row),
            _resident(wz.shape), _resident(wx.shape), _resident(wdt.shape),
            _resident(conv_w.shape), _resident(conv_b.shape), _resident(dt_bias.shape),
        ],
        out_specs=[
            pl.BlockSpec((tm, SSD_D_INNER), row),
            pl.BlockSpec((tm, SSD_CONV_DIM), row),
            pl.BlockSpec((tm, LANES), row),
        ],
        out_shape=[
            jax.ShapeDtypeStruct((n, SSD_D_INNER), BF16),
            jax.ShapeDtypeStruct((n, SSD_CONV_DIM), BF16),
            jax.ShapeDtypeStruct((n, LANES), F32),
        ],
        scratch_shapes=[pltpu.VMEM((8 + tm, cn), F32), pltpu.VMEM((8, SSD_CONV_DIM), F32)],
        compiler_params=pltpu.CompilerParams(
            dimension_semantics=("arbitrary", "arbitrary"), vmem_limit_bytes=VMEM_LIMIT),
        name="ssd_inproj_conv",
    )(x2, wz, wx, wdt, conv_w, conv_b, dt_bias)


def _ssd_scan_kernel(xbc_ref, z_ref, dt_ref, arow_ref, dexp_ref, nw_ref, e128_ref, e64_ref,
                     y_ref, state_ref):
    L = SSD_L
    G, R, P, NS = SSD_N_GROUPS, SSD_HEADS_PER_GROUP, SSD_HEAD_DIM, SSD_D_STATE

    @pl.when(pl.program_id(1) == 0)
    def _():
        state_ref[...] = jnp.zeros_like(state_ref)

    row_i = lax.broadcasted_iota(I32, (L, L), 0)
    col_i = lax.broadcasted_iota(I32, (L, L), 1)
    tril = jnp.where(col_i <= row_i, 1.0, 0.0).astype(BF16)
    eye = jnp.where(col_i == row_i, 1.0, 0.0)
    neg = jnp.where(col_i <= row_i, 0.0, -jnp.inf)

    dt = dt_ref[...]
    a = dt * arow_ref[...]
    a_parts = _split_bf16(a, 3)
    acum = _dot(tril, a_parts[0]) + _dot(tril, a_parts[1]) + _dot(tril, a_parts[2])

    ac_parts = _split_bf16(acum, 3)
    dt_parts = _split_bf16(dt, 2)
    lhs_t = jnp.concatenate(ac_parts, axis=0)
    ex_t = _dot(lhs_t, e128_ref[...])
    acum_t = ex_t[0:L] + ex_t[L:2 * L] + ex_t[2 * L:3 * L]
    lhs_p = jnp.concatenate(ac_parts + dt_parts, axis=0)
    ex_p = _dot(lhs_p, e64_ref[...])
    acum_p = ex_p[0:L] + ex_p[L:2 * L] + ex_p[2 * L:3 * L]
    dt_p = ex_p[3 * L:4 * L] + ex_p[4 * L:5 * L]

    x = xbc_ref[:, 0:SSD_D_INNER].astype(F32)
    last_p = acum_p[L - 1:L, :]
    xdt = (x * dt_p).astype(BF16)
    xw = (x * (jnp.exp(last_p - acum_p) * dt_p)).astype(BF16)
    exp_a_p = jnp.exp(acum_p)
    exp_last_p = jnp.exp(last_p)

    lane = lax.broadcasted_iota(I32, (L, LANES), 1)
    lo_half = lane < P

    y_parts = []
    for g in range(G):
        b_g = xbc_ref[:, SSD_D_INNER + g * NS:SSD_D_INNER + (g + 1) * NS]
        c_g = xbc_ref[:, SSD_D_INNER + SSD_BC_DIM + g * NS:SSD_D_INNER + SSD_BC_DIM + (g + 1) * NS]
        scores = _dot_nt(c_g, b_g)
        gsl = slice(g * R * P, (g + 1) * R * P)
        y_off = _dot(c_g, state_ref[g].astype(BF16)) * exp_a_p[:, gsl]
        y_diag = []
        for pr in range(R // 2):
            h0 = g * R + 2 * pr
            m_pair = []
            for h in (h0, h0 + 1):
                blk = acum_t[:, h * L:(h + 1) * L]
                rowv = jnp.sum(blk * eye, axis=0, keepdims=True)
                m_pair.append((scores * jnp.exp(blk - rowv + neg)).astype(BF16))
            xs = xdt[:, h0 * P:(h0 + 2) * P]
            zero = jnp.zeros_like(xs)
            bd = jnp.concatenate([jnp.where(lo_half, xs, zero), jnp.where(lo_half, zero, xs)], axis=0)
            y_diag.append(_dot(jnp.concatenate(m_pair, axis=1), bd))
        y_parts.append(jnp.concatenate(y_diag, axis=1) + y_off)
        state_ref[g] = state_ref[g] * exp_last_p[:, gsl] + _dot_tn(b_g, xw[:, gsl])

    y = jnp.concatenate(y_parts, axis=1) + x * dexp_ref[...]
    yz = y * _silu(z_ref[...].astype(F32))
    outs = []
    for g in range(G):
        blk = yz[:, g * R * P:(g + 1) * R * P]
        ms = jnp.mean(blk * blk, axis=-1, keepdims=True)
        outs.append(blk * lax.rsqrt(ms + RMS_EPS))
    y_ref[...] = (jnp.concatenate(outs, axis=1) * nw_ref[...]).astype(BF16)


def _ssd_scan(xbc, z, dt, a_row, d_exp, norm_w, e128, e64, *, bsz, seq):
    n = bsz * seq
    L = SSD_L
    nt = seq // L
    row = lambda b, j: (b * nt + j, 0)
    return pl.pallas_call(
        _ssd_scan_kernel,
        grid=(bsz, nt),
        in_specs=[
            pl.BlockSpec((L, SSD_CONV_DIM), row),
            pl.BlockSpec((L, SSD_D_INNER), row),
            pl.BlockSpec((L, LANES), row),
            _resident(a_row.shape), _resident(d_exp.shape), _resident(norm_w.shape),
            _resident(e128.shape), _resident(e64.shape),
        ],
        out_specs=pl.BlockSpec((L, SSD_D_INNER), row),
        out_shape=jax.ShapeDtypeStruct((n, SSD_D_INNER), BF16),
        scratch_shapes=[pltpu.VMEM((SSD_N_GROUPS, SSD_D_STATE, SSD_HEADS_PER_GROUP * SSD_HEAD_DIM), F32)],
        compiler_params=pltpu.CompilerParams(
            dimension_semantics=("arbitrary", "arbitrary"), vmem_limit_bytes=VMEM_LIMIT),
        name="ssd_scan_gate_norm",
    )(xbc, z, dt, a_row, d_exp, norm_w, e128, e64)


def _layer_norm(v, g, b):
    mu = jnp.mean(v, axis=-1, keepdims=True)
    c = v - mu
    var = jnp.mean(c * c, axis=-1, keepdims=True)
    return c * lax.rsqrt(var + LN_EPS) * g + b


def _proj_ln_kernel(a_ref, w_ref, res_ref, g_ref, b_ref, o_ref):
    m = _dot(a_ref[...], w_ref[...])
    o_ref[...] = _layer_norm(DEEPNORM_ALPHA * res_ref[...] + m, g_ref[...], b_ref[...])


def _proj_ln(a, w, res, g, b):
    n, k = a.shape
    tm = min(ROW_TILE, n)
    row = lambda i: (i, 0)
    return pl.pallas_call(
        _proj_ln_kernel,
        grid=(n // tm,),
        in_specs=[pl.BlockSpec((tm, k), row), _resident(w.shape), pl.BlockSpec((tm, D_MODEL), row),
                  _resident(g.shape), _resident(b.shape)],
        out_specs=pl.BlockSpec((tm, D_MODEL), row),
        out_shape=jax.ShapeDtypeStruct((n, D_MODEL), F32),
        compiler_params=pltpu.CompilerParams(
            dimension_semantics=("arbitrary",), vmem_limit_bytes=VMEM_LIMIT),
        name="outproj_deepnorm_ln",
    )(a, w, res, g, b)


_PAIRS = ((0, 1), (0, 2), (0, 3), (1, 2), (1, 3), (2, 3))


def _router_kernel(h_ref, rwt_ref, rb_ref, bucket_ref, rank_ref, counts_ref, base_ref, *, tm):
    @pl.when(pl.program_id(0) == 0)
    def _():
        base_ref[...] = jnp.zeros_like(base_ref)

    logits = lax.dot_general(rwt_ref[...], h_ref[...], (((1,), (1,)), ((), ())),
                             preferred_element_type=F32, precision=lax.Precision.HIGHEST)
    sel = 1.0 / (1.0 + jnp.exp(-logits)) + rb_ref[...]
    rows = [sel[e:e + 1, :] for e in range(N_EXPERTS)]

    best_g = jnp.zeros((1, tm), I32)
    best_s = None
    for g in range(N_EXPERT_GROUPS):
        v = rows[4 * g:4 * g + 4]
        gs = None
        for (i, j) in _PAIRS:
            s = v[i] + v[j]
            gs = s if gs is None else jnp.maximum(gs, s)
        if best_s is None:
            best_s = gs
        else:
            better = gs > best_s
            best_g = jnp.where(better, g, best_g)
            best_s = jnp.where(better, gs, best_s)

    v = []
    for i in range(EXPERTS_PER_GROUP):
        acc = rows[i]
        for g in range(1, N_EXPERT_GROUPS):
            acc = jnp.where(best_g == g, rows[4 * g + i], acc)
        v.append(acc)
    keep = []
    for i in range(EXPERTS_PER_GROUP):
        beaten = jnp.zeros((1, tm), I32)
        for j in range(EXPERTS_PER_GROUP):
            if j == i:
                continue
            wins = (v[j] > v[i]) if j > i else (v[j] >= v[i])
            beaten = beaten + jnp.where(wins, 1, 0)
        keep.append(jnp.where(beaten < 2, 1, 0))
    pair = jnp.full((1, tm), N_PAIRS - 1, I32)
    for p in range(N_PAIRS - 2, -1, -1):
        i, j = _PAIRS[p]
        pair = jnp.where(keep[i] * keep[j] > 0, p, pair)
    bucket = best_g * N_PAIRS + pair

    brow = lax.broadcasted_iota(I32, (BUCKET_ROWS, tm), 0)
    onehot = jnp.where(brow == bucket, 1.0, 0.0)
    ti = lax.broadcasted_iota(I32, (tm, tm), 0)
    tj = lax.broadcasted_iota(I32, (tm, tm), 1)
    before = jnp.where(ti < tj, 1.0, 0.0).astype(BF16)
    prefix = _dot(onehot.astype(BF16), before)
    base = base_ref[:, 0:1]
    rank = jnp.sum(onehot * (prefix + base), axis=0, keepdims=True)
    cnt = jnp.sum(onehot, axis=1, keepdims=True)
    base_ref[...] = base_ref[...] + cnt
    bucket_ref[...] = bucket
    rank_ref[...] = rank.astype(I32)
    counts_ref[...] = base_ref[...].astype(I32)


def _router(h, rw_t, rb_col):
    n = h.shape[0]
    tm = min(ROW_TILE, n)
    nt = n // tm
    kern = functools.partial(_router_kernel, tm=tm)
    bucket, rank, counts = pl.pallas_call(
        kern,
        grid=(nt,),
        in_specs=[pl.BlockSpec((tm, D_MODEL), lambda i: (i, 0)), _resident(rw_t.shape), _resident(rb_col.shape)],
        out_specs=[pl.BlockSpec((None, 1, tm), lambda i: (i, 0, 0)),
                   pl.BlockSpec((None, 1, tm), lambda i: (i, 0, 0)),
                   pl.BlockSpec((BUCKET_ROWS, LANES), lambda i: (0, 0))],
        out_shape=[jax.ShapeDtypeStruct((nt, 1, tm), I32), jax.ShapeDtypeStruct((nt, 1, tm), I32),
                   jax.ShapeDtypeStruct((BUCKET_ROWS, LANES), I32)],
        scratch_shapes=[pltpu.VMEM((BUCKET_ROWS, LANES), F32)],
        compiler_params=pltpu.CompilerParams(
            dimension_semantics=("arbitrary",), vmem_limit_bytes=VMEM_LIMIT),
        name="moe_router",
    )(h, rw_t, rb_col)
    return bucket.reshape(n), rank.reshape(n), counts[:N_BUCKETS, 0]


def _rows_to_tiles(ref, v, rows):
    for j in range(ROW_SUB):
        ref[pl.ds(j, rows, stride=ROW_SUB), :] = v[:, j * LANES:(j + 1) * LANES]


def _tiles_to_rows(ref, rows):
    return jnp.concatenate([ref[pl.ds(j, rows, stride=ROW_SUB), :] for j in range(ROW_SUB)], axis=1)


def _row_tile(ref, r):
    return ref.at[pl.ds(pl.multiple_of(r * ROW_SUB, ROW_SUB), ROW_SUB)]


def _scatter_rows_kernel(zstart_ref, zvalid_ref, pos_ref, h_ref, o_ref, buf_ref, zeros_ref, sems, zsem, *, tm):
    def zero_copy(b):
        start = pl.multiple_of(zstart_ref[b] * ROW_SUB, ROW_SUB)
        return pltpu.make_async_copy(zeros_ref, o_ref.at[pl.ds(start, MOE_TILE * ROW_SUB)], zsem)

    @pl.when(pl.program_id(0) == 0)
    def _():
        zeros_ref[...] = jnp.zeros_like(zeros_ref)
        for b in range(2 * N_BUCKETS):
            pl.when(zvalid_ref[b] > 0)(lambda b=b: zero_copy(b).start())
        for b in range(2 * N_BUCKETS):
            pl.when(zvalid_ref[b] > 0)(lambda b=b: zero_copy(b).wait())

    i = pl.program_id(0)
    last = pl.num_programs(0) - 1
    slot = i % 2
    buf = buf_ref.at[slot]

    def wait_rows(s):
        pltpu.make_async_copy(buf_ref.at[s], o_ref.at[pl.ds(0, tm * ROW_SUB)], sems.at[s]).wait()

    @pl.when(i >= 2)
    def _():
        wait_rows(slot)

    _rows_to_tiles(buf, h_ref[...], tm)

    def issue(r, carry):
        pltpu.make_async_copy(_row_tile(buf, r), _row_tile(o_ref, pos_ref[0, r]), sems.at[slot]).start()
        return carry

    lax.fori_loop(0, tm, issue, 0)

    @pl.when(i == last)
    def _():
        wait_rows(slot)

        @pl.when(last >= 1)
        def _():
            wait_rows(1 - slot)


def _scatter_rows(h, pos, zstart, zvalid, n_sorted):
    n = h.shape[0]
    tm = min(ROW_TILE, n)
    nt = n // tm
    kern = functools.partial(_scatter_rows_kernel, tm=tm)
    grid_spec = pltpu.PrefetchScalarGridSpec(
        num_scalar_prefetch=2,
        grid=(nt,),
        in_specs=[pl.BlockSpec((None, 1, tm), lambda i, zs, zv: (i, 0, 0), memory_space=pltpu.SMEM),
                  pl.BlockSpec((tm, D_MODEL), lambda i, zs, zv: (i, 0))],
        out_specs=pl.BlockSpec(memory_space=pl.ANY),
        scratch_shapes=[pltpu.VMEM((2, tm * ROW_SUB, LANES), F32), pltpu.VMEM((MOE_TILE * ROW_SUB, LANES), F32),
                        pltpu.SemaphoreType.DMA((2,)), pltpu.SemaphoreType.DMA],
    )
    return pl.pallas_call(
        kern,
        grid_spec=grid_spec,
        out_shape=jax.ShapeDtypeStruct((n_sorted * ROW_SUB, LANES), F32),
        compiler_params=pltpu.CompilerParams(
            dimension_semantics=("arbitrary",), vmem_limit_bytes=VMEM_LIMIT),
        name="moe_scatter_rows",
    )(zstart, zvalid, pos.reshape(nt, 1, tm), h)


def _expert_kernel(blk_ref, ea_ref, eb_ref, nvalid_ref,
                   x_ref, rwa_ref, rwb_ref, wga_ref, wua_ref, wda_ref, wgb_ref, wub_ref, wdb_ref, o_ref):
    @pl.when(pl.program_id(0) < nvalid_ref[0])
    def _():
        x = _tiles_to_rows(x_ref, MOE_TILE)
        xb = x.astype(BF16)
        aff_a = 1.0 / (1.0 + jnp.exp(-jnp.sum(x * rwa_ref[...], axis=-1, keepdims=True)))
        aff_b = 1.0 / (1.0 + jnp.exp(-jnp.sum(x * rwb_ref[...], axis=-1, keepdims=True)))
        tot = aff_a + aff_b
        h_a = (_silu(_dot(xb, wga_ref[...])) * _dot(xb, wua_ref[...]) * (aff_a / tot)).astype(BF16)
        h_b = (_silu(_dot(xb, wgb_ref[...])) * _dot(xb, wub_ref[...]) * (aff_b / tot)).astype(BF16)
        _rows_to_tiles(o_ref, _dot(h_a, wda_ref[...]) + _dot(h_b, wdb_ref[...]), MOE_TILE)

    @pl.when(pl.program_id(0) >= nvalid_ref[0])
    def _():
        o_ref[...] = jnp.zeros_like(o_ref)


def _expert_mlp(x_sorted, tile_blk, tile_ea, tile_eb, nvalid, rw_rows, wg, wu, wd):
    n_sorted = x_sorted.shape[0] // ROW_SUB
    nt = n_sorted // MOE_TILE
    xmap = lambda i, blk, ea, eb, nv: (blk[i], 0)
    amap = lambda i, blk, ea, eb, nv: (ea[i], 0, 0)
    bmap = lambda i, blk, ea, eb, nv: (eb[i], 0, 0)
    w_in = (None, D_MODEL, D_FF_EXPERT)
    w_out = (None, D_FF_EXPERT, D_MODEL)
    grid_spec = pltpu.PrefetchScalarGridSpec(
        num_scalar_prefetch=4,
        grid=(nt,),
        in_specs=[pl.BlockSpec((MOE_TILE * ROW_SUB, LANES), xmap),
                  pl.BlockSpec((None, 1, D_MODEL), amap), pl.BlockSpec((None, 1, D_MODEL), bmap),
                  pl.BlockSpec(w_in, amap), pl.BlockSpec(w_in, amap), pl.BlockSpec(w_out, amap),
                  pl.BlockSpec(w_in, bmap), pl.BlockSpec(w_in, bmap), pl.BlockSpec(w_out, bmap)],
        out_specs=pl.BlockSpec((MOE_TILE * ROW_SUB, LANES), lambda i, blk, ea, eb, nv: (i, 0)),
    )
    return pl.pallas_call(
        _expert_kernel,
        grid_spec=grid_spec,
        out_shape=jax.ShapeDtypeStruct((n_sorted * ROW_SUB, LANES), F32),
        compiler_params=pltpu.CompilerParams(
            dimension_semantics=("arbitrary",), vmem_limit_bytes=VMEM_LIMIT),
        name="moe_expert_mlp",
    )(tile_blk, tile_ea, tile_eb, nvalid, x_sorted, rw_rows, rw_rows, wg, wu, wd, wg, wu, wd)


def _gather_ln_kernel(pos_ref, pos_next_ref, f_ref, h_ref, g_ref, b_ref, o_ref, buf_ref, sems, *, tm):
    i = pl.program_id(0)
    slot = i % 2

    def fetch(p_ref, s):
        def issue(r, carry):
            pltpu.make_async_copy(_row_tile(f_ref, p_ref[0, r]), _row_tile(buf_ref.at[s], r), sems.at[s]).start()
            return carry
        lax.fori_loop(0, tm, issue, 0)

    @pl.when(i == 0)
    def _():
        fetch(pos_ref, slot)

    @pl.when(i + 1 < pl.num_programs(0))
    def _():
        fetch(pos_next_ref, 1 - slot)

    pltpu.make_async_copy(f_ref.at[pl.ds(0, tm * ROW_SUB)], buf_ref.at[slot], sems.at[slot]).wait()
    f = _tiles_to_rows(buf_ref.at[slot], tm)
    o_ref[...] = _layer_norm(DEEPNORM_ALPHA * h_ref[...] + f, g_ref[...], b_ref[...])


def _gather_ln(f_sorted, pos, h, g, b):
    n = h.shape[0]
    tm = min(ROW_TILE, n)
    nt = n // tm
    kern = functools.partial(_gather_ln_kernel, tm=tm)
    pos3 = pos.reshape(nt, 1, tm)
    return pl.pallas_call(
        kern,
        grid=(nt,),
        in_specs=[pl.BlockSpec((None, 1, tm), lambda i: (i, 0, 0), memory_space=pltpu.SMEM),
                  pl.BlockSpec((None, 1, tm), lambda i: (jnp.minimum(i + 1, nt - 1), 0, 0), memory_space=pltpu.SMEM),
                  pl.BlockSpec(memory_space=pl.ANY),
                  pl.BlockSpec((tm, D_MODEL), lambda i: (i, 0)),
                  _resident(g.shape), _resident(b.shape)],
        out_specs=pl.BlockSpec((tm, D_MODEL), lambda i: (i, 0)),
        out_shape=jax.ShapeDtypeStruct((n, D_MODEL), F32),
        scratch_shapes=[pltpu.VMEM((2, tm * ROW_SUB, LANES), F32), pltpu.SemaphoreType.DMA((2,))],
        compiler_params=pltpu.CompilerParams(
            dimension_semantics=("arbitrary",), vmem_limit_bytes=VMEM_LIMIT),
        name="moe_gather_deepnorm_ln",
    )(pos3, pos3, f_sorted, h, g, b)


def _moe_layer(h, rw_t, rb_col, rw_rows, wg, wu, wd, ln_g, ln_b):
    n = h.shape[0]
    n_tiles = n // MOE_TILE + N_BUCKETS
    n_sorted = n_tiles * MOE_TILE
    bucket, rank, counts = _router(h, rw_t, rb_col)
    tiles_per = (counts + MOE_TILE - 1) // MOE_TILE
    tile_end = jnp.cumsum(tiles_per)
    tile_start = tile_end - tiles_per
    pos = tile_start[bucket] * MOE_TILE + rank
    nvalid = tile_end[-1]
    tail = nvalid + jnp.arange(N_BUCKETS, dtype=I32)
    zstart = jnp.concatenate([jnp.maximum(tile_end - 1, 0), jnp.minimum(tail, n_tiles - 1)]) * MOE_TILE
    zvalid = jnp.concatenate([tiles_per > 0, tail < n_tiles]).astype(I32)
    tidx = jnp.minimum(jnp.arange(n_tiles, dtype=I32), nvalid - 1)
    tb = jnp.sum(tile_end[None, :] <= tidx[:, None], axis=1).astype(I32)
    pair_lo = jnp.array([p[0] for p in _PAIRS], I32)
    pair_hi = jnp.array([p[1] for p in _PAIRS], I32)
    tile_ea = (tb // N_PAIRS) * EXPERTS_PER_GROUP + pair_lo[tb % N_PAIRS]
    tile_eb = (tb // N_PAIRS) * EXPERTS_PER_GROUP + pair_hi[tb % N_PAIRS]
    x_sorted = _scatter_rows(h, pos.astype(I32), zstart.astype(I32), zvalid, n_sorted)
    f_sorted = _expert_mlp(x_sorted, tidx, tile_ea.astype(I32), tile_eb.astype(I32),
                           nvalid.reshape(1).astype(I32), rw_rows, wg, wu, wd)
    return _gather_ln(f_sorted, pos.astype(I32), h, ln_g, ln_b)


def _rot_half_slot(v, lane):
    w = v.shape[1]
    fwd = pltpu.roll(v, MLA_ROPE // 2, 1)
    bwd = pltpu.roll(v, w - MLA_ROPE // 2, 1)
    first = (lane >= MLA_NOPE) & (lane < MLA_NOPE + MLA_ROPE // 2)
    second = (lane >= MLA_NOPE + MLA_ROPE // 2) & (lane < MLA_NOPE + MLA_ROPE)
    return jnp.where(first, -bwd, jnp.where(second, fwd, 0.0))


def _mla_proj_kernel(x_ref, wd_ref, wkr_ref, qn_ref, kvn_ref, wuq_ref, wuk_ref, wuv_ref, cos_ref, sin_ref,
                     q_ref, k_ref, v_ref, *, tm):
    xb = x_ref[...].astype(BF16)
    down = _dot(xb, wd_ref[...])
    c_q = down[:, 0:MLA_Q_RANK]
    c_kv = down[:, MLA_Q_RANK:MLA_Q_RANK + MLA_KV_RANK]
    c_q = c_q * lax.rsqrt(jnp.mean(c_q * c_q, axis=-1, keepdims=True) + RMS_EPS) * qn_ref[...]
    c_kv = c_kv * lax.rsqrt(jnp.mean(c_kv * c_kv, axis=-1, keepdims=True) + RMS_EPS) * kvn_ref[...]
    c_q = c_q.astype(BF16)
    c_kv = c_kv.astype(BF16)

    cos = cos_ref[...]
    sin = sin_ref[...]
    lane = lax.broadcasted_iota(I32, (tm, HEAD_SLOT), 1)
    kr = _dot(xb, wkr_ref[...])
    kr = kr * cos + _rot_half_slot(kr, lane) * sin
    scale = 1.0 / math.sqrt(MLA_NOPE + MLA_ROPE)
    for h in range(MLA_N_HEADS):
        sl = slice(h * HEAD_SLOT, (h + 1) * HEAD_SLOT)
        q = _dot(c_q, wuq_ref[:, sl])
        q = (q * cos + _rot_half_slot(q, lane) * sin) * scale
        q_ref[:, sl] = q.astype(BF16)
        k_ref[:, sl] = (_dot(c_kv, wuk_ref[:, sl]) + kr).astype(BF16)
        v_ref[:, sl] = _dot(c_kv, wuv_ref[:, sl]).astype(BF16)


def _mla_proj(h2, wd, wkr, qn, kvn, wuq, wuk, wuv, cos_t, sin_t, *, bsz, seq):
    n = bsz * seq
    tm = min(ROW_TILE, seq)
    nt = seq // tm
    row = lambda b, j: (b * nt + j, 0)
    tab = lambda b, j: (j, 0)
    wide = MLA_N_HEADS * HEAD_SLOT
    kern = functools.partial(_mla_proj_kernel, tm=tm)
    return pl.pallas_call(
        kern,
        grid=(bsz, nt),
        in_specs=[pl.BlockSpec((tm, D_MODEL), row), _resident(wd.shape), _resident(wkr.shape),
                  _resident(qn.shape), _resident(kvn.shape), _resident(wuq.shape), _resident(wuk.shape),
                  _resident(wuv.shape), pl.BlockSpec((tm, HEAD_SLOT), tab), pl.BlockSpec((tm, HEAD_SLOT), tab)],
        out_specs=[pl.BlockSpec((tm, wide), row)] * 3,
        out_shape=[jax.ShapeDtypeStruct((n, wide), BF16)] * 3,
        compiler_params=pltpu.CompilerParams(
            dimension_semantics=("arbitrary", "arbitrary"), vmem_limit_bytes=VMEM_LIMIT),
        name="mla_proj_rope",
    )(h2, wd, wkr, qn, kvn, wuq, wuk, wuv, cos_t, sin_t)


def _attn_kernel(q_ref, k_ref, v_ref, o_ref, *, tq):
    qi = pl.program_id(2)
    q = q_ref[...]

    def step(s, m, l, acc, k0):
        m_new = jnp.maximum(m, jnp.max(s, axis=-1, keepdims=True))
        p = jnp.exp(s - m_new)
        corr = jnp.exp(m - m_new)
        l_new = corr * l + jnp.sum(p, axis=-1, keepdims=True)
        acc_new = corr * acc + _dot(p.astype(BF16), v_ref[pl.ds(k0, tq), :])
        return m_new, l_new, acc_new

    def body(j, carry):
        m, l, acc = carry
        k0 = pl.multiple_of(j * tq, tq)
        s = _dot_nt(q, k_ref[pl.ds(k0, tq), :])
        return step(s, m, l, acc, k0)

    init = (jnp.full((tq, 1), -jnp.inf, F32), jnp.zeros((tq, 1), F32), jnp.zeros((tq, HEAD_SLOT), F32))
    m, l, acc = lax.fori_loop(0, qi, body, init)
    k0 = pl.multiple_of(qi * tq, tq)
    s = _dot_nt(q, k_ref[pl.ds(k0, tq), :])
    qc = lax.broadcasted_iota(I32, (tq, tq), 0) // ATTN_CHUNK
    kc = lax.broadcasted_iota(I32, (tq, tq), 1) // ATTN_CHUNK
    s = jnp.where(kc <= qc, s, -jnp.inf)
    m, l, acc = step(s, m, l, acc, k0)
    o_ref[...] = (acc / l).astype(BF16)


def _attention(q, k, v, *, bsz, seq):
    n = bsz * seq
    tq = min(ROW_TILE, seq)
    nq = seq // tq
    kern = functools.partial(_attn_kernel, tq=tq)
    return pl.pallas_call(
        kern,
        grid=(bsz, MLA_N_HEADS, nq),
        in_specs=[pl.BlockSpec((tq, HEAD_SLOT), lambda b, h, i: (b * nq + i, h)),
                  pl.BlockSpec((seq, HEAD_SLOT), lambda b, h, i: (b, h)),
                  pl.BlockSpec((seq, HEAD_SLOT), lambda b, h, i: (b, h))],
        out_specs=pl.BlockSpec((tq, HEAD_SLOT), lambda b, h, i: (b * nq + i, h)),
        out_shape=jax.ShapeDtypeStruct((n, MLA_N_HEADS * HEAD_SLOT), BF16),
        compiler_params=pltpu.CompilerParams(
            dimension_semantics=("arbitrary", "arbitrary", "arbitrary"), vmem_limit_bytes=VMEM_LIMIT),
        name="mla_flash_attention",
    )(q, k, v)


def _head_slots(w, head_dim, n_heads):
    k = w.shape[0]
    w3 = w.reshape(k, n_heads, head_dim)
    return jnp.pad(w3, ((0, 0), (0, 0), (0, HEAD_SLOT - head_dim))).reshape(k, n_heads * HEAD_SLOT)


def kernel(x, ssd_w_in, ssd_conv_w, ssd_conv_b, ssd_dt_bias, ssd_a_log, ssd_d, ssd_norm_w, ssd_w_out, mla_w_down, mla_q_norm, mla_w_uq, mla_kv_norm, mla_w_ukv, mla_w_out, router_w, router_bias, moe_w_gate, moe_w_up, moe_w_down, ln_mix_g, ln_mix_b, ln_ffn_g, ln_ffn_b):
    bsz, seq, _ = x.shape
    n = bsz * seq
    h = x.reshape(n, D_MODEL)

    rw_t = router_w.T
    rw_rows = rw_t.reshape(N_EXPERTS, 1, D_MODEL)
    rb_col = router_bias.reshape(N_EXPERTS, 1)
    head_of_lane_p = jnp.arange(SSD_D_INNER) // SSD_HEAD_DIM
    e64 = (jnp.arange(LANES)[:, None] == head_of_lane_p[None, :]).astype(BF16)
    e128 = (jnp.arange(LANES)[:, None] == (jnp.arange(SSD_N_HEADS * SSD_L) // SSD_L)[None, :]).astype(BF16)

    for i in range(DEPTH):
        j = i // N_MIXERS
        if i % N_MIXERS == 0:
            w_in = ssd_w_in[j]
            wz = w_in[:, :SSD_D_INNER].astype(BF16)
            wx = w_in[:, SSD_D_INNER:SSD_D_INNER + SSD_CONV_DIM].astype(BF16)
            pad_h = LANES - SSD_N_HEADS
            wdt = jnp.pad(w_in[:, SSD_D_INNER + SSD_CONV_DIM:], ((0, 0), (0, pad_h))).astype(BF16)
            dt_bias = jnp.pad(ssd_dt_bias[j], (0, pad_h)).reshape(1, LANES)
            a_row = jnp.pad(-jnp.exp(ssd_a_log[j]), (0, pad_h)).reshape(1, LANES)
            d_exp = ssd_d[j][head_of_lane_p].reshape(1, SSD_D_INNER)
            z, xbc, dt = _ssd_inproj(h, wz, wx, wdt, ssd_conv_w[j], ssd_conv_b[j].reshape(1, SSD_CONV_DIM),
                                     dt_bias, bsz=bsz, seq=seq)
            y = _ssd_scan(xbc, z, dt, a_row, d_exp, ssd_norm_w[j].reshape(1, SSD_D_INNER), e128, e64,
                          bsz=bsz, seq=seq)
            w_o = ssd_w_out[j].astype(BF16)
        else:
            w_down = mla_w_down[j]
            wd = w_down[:, :MLA_Q_RANK + MLA_KV_RANK].astype(BF16)
            wkr = jnp.pad(w_down[:, MLA_Q_RANK + MLA_KV_RANK:],
                          ((0, 0), (MLA_NOPE, HEAD_SLOT - MLA_NOPE - MLA_ROPE))).astype(BF16)
            wuq = _head_slots(mla_w_uq[j], MLA_NOPE + MLA_ROPE, MLA_N_HEADS).astype(BF16)
            wukv = mla_w_ukv[j].reshape(MLA_KV_RANK, MLA_N_HEADS, MLA_NOPE + MLA_V)
            wuk = _head_slots(wukv[:, :, :MLA_NOPE].reshape(MLA_KV_RANK, -1), MLA_NOPE, MLA_N_HEADS).astype(BF16)
            wuv = _head_slots(wukv[:, :, MLA_NOPE:].reshape(MLA_KV_RANK, -1), MLA_V, MLA_N_HEADS).astype(BF16)
            inv = ROPE_THETA ** (-jnp.arange(0, MLA_ROPE, 2, dtype=F32) / MLA_ROPE)
            ang = jnp.arange(seq, dtype=F32)[:, None] * inv[None, :]
            ones = jnp.ones((seq, MLA_NOPE), F32)
            zeros = jnp.zeros((seq, HEAD_SLOT - MLA_NOPE - MLA_ROPE), F32)
            cos_t = jnp.concatenate([ones, jnp.cos(ang), jnp.cos(ang), zeros], axis=1)
            sin_t = jnp.concatenate([0.0 * ones, jnp.sin(ang), jnp.sin(ang), zeros], axis=1)
            q, k, v = _mla_proj(h, wd, wkr, mla_q_norm[j].reshape(1, MLA_Q_RANK),
                                mla_kv_norm[j].reshape(1, MLA_KV_RANK), wuq, wuk, wuv, cos_t, sin_t,
                                bsz=bsz, seq=seq)
            y = _attention(q, k, v, bsz=bsz, seq=seq)
            w_o3 = mla_w_out[j].reshape(MLA_N_HEADS, MLA_V, D_MODEL)
            w_o = jnp.pad(w_o3, ((0, 0), (0, HEAD_SLOT - MLA_V), (0, 0))).reshape(-1, D_MODEL).astype(BF16)
        h = _proj_ln(y, w_o, h, ln_mix_g[i].reshape(1, D_MODEL), ln_mix_b[i].reshape(1, D_MODEL))
        h = _moe_layer(h, rw_t, rb_col, rw_rows, moe_w_gate[i].astype(BF16), moe_w_up[i].astype(BF16),
                       moe_w_down[i].astype(BF16), ln_ffn_g[i].reshape(1, D_MODEL), ln_ffn_b[i].reshape(1, D_MODEL))
    return h.reshape(bsz, seq, D_MODEL)
```

```python
import functools
import math

import jax
import jax.numpy as jnp
from jax import lax
from jax.experimental import pallas as pl
from jax.experimental.pallas import tpu as pltpu

F32 = jnp.float32
BF16 = jnp.bfloat16
I32 = jnp.int32

D_MODEL = 1024
DEPTH = 2
N_MIXERS = 2

SSD_D_INNER = 2048
SSD_HEAD_DIM = 64
SSD_N_HEADS = 32
SSD_N_GROUPS = 8
SSD_HEADS_PER_GROUP = 4
SSD_D_STATE = 128
SSD_CONV_W = 4
SSD_BC_DIM = SSD_N_GROUPS * SSD_D_STATE
SSD_CONV_DIM = SSD_D_INNER + 2 * SSD_BC_DIM

MLA_N_HEADS = 16
MLA_Q_RANK = 384
MLA_KV_RANK = 256
MLA_NOPE = 64
MLA_ROPE = 32
MLA_V = 64
ROPE_THETA = 10000.0
ATTN_CHUNK = 64

N_EXPERTS = 16
N_EXPERT_GROUPS = 4
EXPERTS_PER_GROUP = 4
D_FF_EXPERT = 512
N_PAIRS = 6
N_BUCKETS = N_EXPERT_GROUPS * N_PAIRS
BUCKET_ROWS = 32

DEEPNORM_ALPHA = (2.0 * DEPTH) ** 0.25
LN_EPS = 1e-5
RMS_EPS = 1e-6

LANES = 128
ROW_SUB = D_MODEL // LANES
HEAD_SLOT = 128
SSD_L = 128
VMEM_LIMIT = 56 * 1024 * 1024

ROW_TILE = 512
MOE_TILE = 256


def _silu(v):
    return v / (1.0 + jnp.exp(-v))


def _softplus(v):
    return jnp.maximum(v, 0.0) + jnp.log1p(jnp.exp(-jnp.abs(v)))


def _dot(a, b):
    return jnp.dot(a, b, preferred_element_type=F32)


def _dot_nt(a, b):
    return lax.dot_general(a, b, (((1,), (1,)), ((), ())), preferred_element_type=F32)


def _dot_tn(a, b):
    return lax.dot_general(a, b, (((0,), (0,)), ((), ())), preferred_element_type=F32)


def _split_bf16(v, terms):
    parts = []
    rem = v
    for _ in range(terms):
        p = rem.astype(BF16)
        parts.append(p)
        rem = rem - p.astype(F32)
    return parts


def _resident(shape):
    nd = len(shape)
    return pl.BlockSpec(shape, lambda *_: (0,) * nd)


def _inproj_kernel(x_ref, wz_ref, wx_ref, wdt_ref, cw_ref, cb_ref, dtb_ref,
                   z_ref, xbc_ref, dt_ref, buf_ref, carry_ref, *, tm, cn):
    @pl.when(pl.program_id(1) == 0)
    def _():
        carry_ref[...] = jnp.zeros_like(carry_ref)

    xb = x_ref[...].astype(BF16)
    for c in range(SSD_D_INNER // cn):
        sl = slice(c * cn, (c + 1) * cn)
        z_ref[:, sl] = _dot(xb, wz_ref[:, sl]).astype(BF16)
    dt_ref[...] = _softplus(_dot(xb, wdt_ref[...]) + dtb_ref[...])
    for c in range(SSD_CONV_DIM // cn):
        sl = slice(c * cn, (c + 1) * cn)
        r = _dot(xb, wx_ref[:, sl])
        buf_ref[0:8, :] = carry_ref[:, sl]
        buf_ref[8:8 + tm, :] = r
        carry_ref[:, sl] = r[tm - 8:tm, :]
        w = cw_ref[:, sl]
        y = (cb_ref[:, sl]
             + w[3:4, :] * r
             + w[2:3, :] * buf_ref[7:7 + tm, :]
             + w[1:2, :] * buf_ref[6:6 + tm, :]
             + w[0:1, :] * buf_ref[5:5 + tm, :])
        xbc_ref[:, sl] = _silu(y).astype(BF16)


def _ssd_inproj(x2, wz, wx, wdt, conv_w, conv_b, dt_bias, *, bsz, seq):
    n = bsz * seq
    tm = min(ROW_TILE, seq)
    cn = 512
    nt = seq // tm
    row = lambda b, j: (b * nt + j, 0)
    kern = functools.partial(_inproj_kernel, tm=tm, cn=cn)
    return pl.pallas_call(
        kern,
        grid=(bsz, nt),
        in_specs=[
            pl.BlockSpec((tm, D_MODEL), row),
            _resident(wz.shape), _resident(wx.shape), _resident(wdt.shape),
            _resident(conv_w.shape), _resident(conv_b.shape), _resident(dt_bias.shape),
        ],
        out_specs=[
            pl.BlockSpec((tm, SSD_D_INNER), row),
            pl.BlockSpec((tm, SSD_CONV_DIM), row),
            pl.BlockSpec((tm, LANES), row),
        ],
        out_shape=[
            jax.ShapeDtypeStruct((n, SSD_D_INNER), BF16),
            jax.ShapeDtypeStruct((n, SSD_CONV_DIM), BF16),
            jax.ShapeDtypeStruct((n, LANES), F32),
        ],
        scratch_shapes=[pltpu.VMEM((8 + tm, cn), F32), pltpu.VMEM((8, SSD_CONV_DIM), F32)],
        compiler_params=pltpu.CompilerParams(
            dimension_semantics=("arbitrary", "arbitrary"), vmem_limit_bytes=VMEM_LIMIT),
        name="ssd_inproj_conv",
    )(x2, wz, wx, wdt, conv_w, conv_b, dt_bias)


def _ssd_scan_kernel(xbc_ref, z_ref, dt_ref, arow_ref, dexp_ref, nw_ref, e128_ref, e64_ref,
                     y_ref, state_ref):
    L = SSD_L
    G, R, P, NS = SSD_N_GROUPS, SSD_HEADS_PER_GROUP, SSD_HEAD_DIM, SSD_D_STATE

    @pl.when(pl.program_id(1) == 0)
    def _():
        state_ref[...] = jnp.zeros_like(state_ref)

    row_i = lax.broadcasted_iota(I32, (L, L), 0)
    col_i = lax.broadcasted_iota(I32, (L, L), 1)
    tril = jnp.where(col_i <= row_i, 1.0, 0.0).astype(BF16)
    eye = jnp.where(col_i == row_i, 1.0, 0.0)
    neg = jnp.where(col_i <= row_i, 0.0, -jnp.inf)

    dt = dt_ref[...]
    a = dt * arow_ref[...]
    a_parts = _split_bf16(a, 3)
    acum = _dot(tril, a_parts[0]) + _dot(tril, a_parts[1]) + _dot(tril, a_parts[2])

    ac_parts = _split_bf16(acum, 3)
    dt_parts = _split_bf16(dt, 2)
    lhs_t = jnp.concatenate(ac_parts, axis=0)
    ex_t = _dot(lhs_t, e128_ref[...])
    acum_t = ex_t[0:L] + ex_t[L:2 * L] + ex_t[2 * L:3 * L]
    lhs_p = jnp.concatenate(ac_parts + dt_parts, axis=0)
    ex_p = _dot(lhs_p, e64_ref[...])
    acum_p = ex_p[0:L] + ex_p[L:2 * L] + ex_p[2 * L:3 * L]
    dt_p = ex_p[3 * L:4 * L] + ex_p[4 * L:5 * L]

    x = xbc_ref[:, 0:SSD_D_INNER].astype(F32)
    last_p = acum_p[L - 1:L, :]
    xdt = (x * dt_p).astype(BF16)
    xw = (x * (jnp.exp(last_p - acum_p) * dt_p)).astype(BF16)
    exp_a_p = jnp.exp(acum_p)
    exp_last_p = jnp.exp(last_p)

    lane = lax.broadcasted_iota(I32, (L, LANES), 1)
    lo_half = lane < P

    y_parts = []
    for g in range(G):
        b_g = xbc_ref[:, SSD_D_INNER + g * NS:SSD_D_INNER + (g + 1) * NS]
        c_g = xbc_ref[:, SSD_D_INNER + SSD_BC_DIM + g * NS:SSD_D_INNER + SSD_BC_DIM + (g + 1) * NS]
        scores = _dot_nt(c_g, b_g)
        gsl = slice(g * R * P, (g + 1) * R * P)
        y_off = _dot(c_g, state_ref[g].astype(BF16)) * exp_a_p[:, gsl]
        y_diag = []
        for pr in range(R // 2):
            h0 = g * R + 2 * pr
            m_pair = []
            for h in (h0, h0 + 1):
                blk = acum_t[:, h * L:(h + 1) * L]
                rowv = jnp.sum(blk * eye, axis=0, keepdims=True)
                m_pair.append((scores * jnp.exp(blk - rowv + neg)).astype(BF16))
            xs = xdt[:, h0 * P:(h0 + 2) * P]
            zero = jnp.zeros_like(xs)
            bd = jnp.concatenate([jnp.where(lo_half, xs, zero), jnp.where(lo_half, zero, xs)], axis=0)
            y_diag.append(_dot(jnp.concatenate(m_pair, axis=1), bd))
        y_parts.append(jnp.concatenate(y_diag, axis=1) + y_off)
        state_ref[g] = state_ref[g] * exp_last_p[:, gsl] + _dot_tn(b_g, xw[:, gsl])

    y = jnp.concatenate(y_parts, axis=1) + x * dexp_ref[...]
    yz = y * _silu(z_ref[...].astype(F32))
    outs = []
    for g in range(G):
        blk = yz[:, g * R * P:(g + 1) * R * P]
        ms = jnp.mean(blk * blk, axis=-1, keepdims=True)
        outs.append(blk * lax.rsqrt(ms + RMS_EPS))
    y_ref[...] = (jnp.concatenate(outs, axis=1) * nw_ref[...]).astype(BF16)


def _ssd_scan(xbc, z, dt, a_row, d_exp, norm_w, e128, e64, *, bsz, seq):
    n = bsz * seq
    L = SSD_L
    nt = seq // L
    row = lambda b, j: (b * nt + j, 0)
    return pl.pallas_call(
        _ssd_scan_kernel,
        grid=(bsz, nt),
        in_specs=[
            pl.BlockSpec((L, SSD_CONV_DIM), row),
            pl.BlockSpec((L, SSD_D_INNER), row),
            pl.BlockSpec((L, LANES), row),
            _resident(a_row.shape), _resident(d_exp.shape), _resident(norm_w.shape),
            _resident(e128.shape), _resident(e64.shape),
        ],
        out_specs=pl.BlockSpec((L, SSD_D_INNER), row),
        out_shape=jax.ShapeDtypeStruct((n, SSD_D_INNER), BF16),
        scratch_shapes=[pltpu.VMEM((SSD_N_GROUPS, SSD_D_STATE, SSD_HEADS_PER_GROUP * SSD_HEAD_DIM), F32)],
        compiler_params=pltpu.CompilerParams(
            dimension_semantics=("arbitrary", "arbitrary"), vmem_limit_bytes=VMEM_LIMIT),
        name="ssd_scan_gate_norm",
    )(xbc, z, dt, a_row, d_exp, norm_w, e128, e64)


def _layer_norm(v, g, b):
    mu = jnp.mean(v, axis=-1, keepdims=True)
    c = v - mu
    var = jnp.mean(c * c, axis=-1, keepdims=True)
    return c * lax.rsqrt(var + LN_EPS) * g + b


def _proj_ln_kernel(a_ref, w_ref, res_ref, g_ref, b_ref, o_ref):
    m = _dot(a_ref[...], w_ref[...])
    o_ref[...] = _layer_norm(DEEPNORM_ALPHA * res_ref[...] + m, g_ref[...], b_ref[...])


def _proj_ln(a, w, res, g, b):
    n, k = a.shape
    tm = min(ROW_TILE, n)
    row = lambda i: (i, 0)
    return pl.pallas_call(
        _proj_ln_kernel,
        grid=(n // tm,),
        in_specs=[pl.BlockSpec((tm, k), row), _resident(w.shape), pl.BlockSpec((tm, D_MODEL), row),
                  _resident(g.shape), _resident(b.shape)],
        out_specs=pl.BlockSpec((tm, D_MODEL), row),
        out_shape=jax.ShapeDtypeStruct((n, D_MODEL), F32),
        compiler_params=pltpu.CompilerParams(
            dimension_semantics=("arbitrary",), vmem_limit_bytes=VMEM_LIMIT),
        name="outproj_deepnorm_ln",
    )(a, w, res, g, b)


_PAIRS = ((0, 1), (0, 2), (0, 3), (1, 2), (1, 3), (2, 3))


def _router_kernel(h_ref, rwt_ref, rb_ref, bucket_ref, rank_ref, counts_ref, base_ref, *, tm):
    @pl.when(pl.program_id(0) == 0)
    def _():
        base_ref[...] = jnp.zeros_like(base_ref)

    wp = _split_bf16(rwt_ref[...], 3)
    hp = _split_bf16(h_ref[...], 3)
    logits = None
    for a, b in ((2, 0), (1, 1), (0, 2), (1, 0), (0, 1), (0, 0)):
        t = _dot_nt(wp[a], hp[b])
        logits = t if logits is None else logits + t
    sel = 1.0 / (1.0 + jnp.exp(-logits)) + rb_ref[...]
    rows = [sel[e:e + 1, :] for e in range(N_EXPERTS)]

    best_g = jnp.zeros((1, tm), I32)
    best_s = None
    for g in range(N_EXPERT_GROUPS):
        v = rows[4 * g:4 * g + 4]
        gs = None
        for (i, j) in _PAIRS:
            s = v[i] + v[j]
            gs = s if gs is None else jnp.maximum(gs, s)
        if best_s is None:
            best_s = gs
        else:
            better = gs > best_s
            best_g = jnp.where(better, g, best_g)
            best_s = jnp.where(better, gs, best_s)

    v = []
    for i in range(EXPERTS_PER_GROUP):
        acc = rows[i]
        for g in range(1, N_EXPERT_GROUPS):
            acc = jnp.where(best_g == g, rows[4 * g + i], acc)
        v.append(acc)
    keep = []
    for i in range(EXPERTS_PER_GROUP):
        beaten = jnp.zeros((1, tm), I32)
        for j in range(EXPERTS_PER_GROUP):
            if j == i:
                continue
            wins = (v[j] > v[i]) if j > i else (v[j] >= v[i])
            beaten = beaten + jnp.where(wins, 1, 0)
        keep.append(jnp.where(beaten < 2, 1, 0))
    pair = jnp.full((1, tm), N_PAIRS - 1, I32)
    for p in range(N_PAIRS - 2, -1, -1):
        i, j = _PAIRS[p]
        pair = jnp.where(keep[i] * keep[j] > 0, p, pair)
    bucket = best_g * N_PAIRS + pair

    brow = lax.broadcasted_iota(I32, (BUCKET_ROWS, tm), 0)
    onehot = jnp.where(brow == bucket, 1.0, 0.0)
    ti = lax.broadcasted_iota(I32, (tm, tm), 0)
    tj = lax.broadcasted_iota(I32, (tm, tm), 1)
    before = jnp.where(ti < tj, 1.0, 0.0).astype(BF16)
    prefix = _dot(onehot.astype(BF16), before)
    base = base_ref[:, 0:1]
    rank = jnp.sum(onehot * (prefix + base), axis=0, keepdims=True)
    cnt = jnp.sum(onehot, axis=1, keepdims=True)
    base_ref[...] = base_ref[...] + cnt
    bucket_ref[...] = bucket
    rank_ref[...] = rank.astype(I32)
    counts_ref[...] = base_ref[...].astype(I32)


def _router(h, rw_t, rb_col):
    n = h.shape[0]
    tm = min(ROW_TILE, n)
    nt = n // tm
    kern = functools.partial(_router_kernel, tm=tm)
    bucket, rank, counts = pl.pallas_call(
        kern,
        grid=(nt,),
        in_specs=[pl.BlockSpec((tm, D_MODEL), lambda i: (i, 0)), _resident(rw_t.shape), _resident(rb_col.shape)],
        out_specs=[pl.BlockSpec((None, 1, tm), lambda i: (i, 0, 0)),
                   pl.BlockSpec((None, 1, tm), lambda i: (i, 0, 0)),
                   pl.BlockSpec((BUCKET_ROWS, LANES), lambda i: (0, 0))],
        out_shape=[jax.ShapeDtypeStruct((nt, 1, tm), I32), jax.ShapeDtypeStruct((nt, 1, tm), I32),
                   jax.ShapeDtypeStruct((BUCKET_ROWS, LANES), I32)],
        scratch_shapes=[pltpu.VMEM((BUCKET_ROWS, LANES), F32)],
        compiler_params=pltpu.CompilerParams(
            dimension_semantics=("arbitrary",), vmem_limit_bytes=VMEM_LIMIT),
        name="moe_router",
    )(h, rw_t, rb_col)
    return bucket.reshape(n), rank.reshape(n), counts[:N_BUCKETS, 0]


def _rows_to_tiles(ref, v, rows):
    for j in range(ROW_SUB):
        ref[pl.ds(j, rows, stride=ROW_SUB), :] = v[:, j * LANES:(j + 1) * LANES]


def _tiles_to_rows(ref, rows):
    return jnp.concatenate([ref[pl.ds(j, rows, stride=ROW_SUB), :] for j in range(ROW_SUB)], axis=1)


def _row_tile(ref, r):
    return ref.at[pl.ds(pl.multiple_of(r * ROW_SUB, ROW_SUB), ROW_SUB)]


def _scatter_rows_kernel(zstart_ref, zvalid_ref, pos_ref, h_ref, o_ref, buf_ref, zeros_ref, sems, zsem, *, tm):
    def zero_copy(b):
        start = pl.multiple_of(zstart_ref[b] * ROW_SUB, ROW_SUB)
        return pltpu.make_async_copy(zeros_ref, o_ref.at[pl.ds(start, MOE_TILE * ROW_SUB)], zsem)

    @pl.when(pl.program_id(0) == 0)
    def _():
        zeros_ref[...] = jnp.zeros_like(zeros_ref)
        for b in range(2 * N_BUCKETS):
            pl.when(zvalid_ref[b] > 0)(lambda b=b: zero_copy(b).start())
        for b in range(2 * N_BUCKETS):
            pl.when(zvalid_ref[b] > 0)(lambda b=b: zero_copy(b).wait())

    i = pl.program_id(0)
    last = pl.num_programs(0) - 1
    slot = i % 2
    buf = buf_ref.at[slot]

    def wait_rows(s):
        pltpu.make_async_copy(buf_ref.at[s], o_ref.at[pl.ds(0, tm * ROW_SUB)], sems.at[s]).wait()

    @pl.when(i >= 2)
    def _():
        wait_rows(slot)

    _rows_to_tiles(buf, h_ref[...], tm)

    def issue(r2, carry):
        for pri in range(2):
            r = 2 * r2 + pri
            pltpu.make_async_copy(_row_tile(buf, r), _row_tile(o_ref, pos_ref[0, r]),
                                  sems.at[slot]).start(priority=pri)
        return carry

    lax.fori_loop(0, tm // 2, issue, 0)

    @pl.when(i == last)
    def _():
        wait_rows(slot)

        @pl.when(last >= 1)
        def _():
            wait_rows(1 - slot)


def _scatter_rows(h, pos, zstart, zvalid, n_sorted):
    n = h.shape[0]
    tm = min(ROW_TILE, n)
    nt = n // tm
    kern = functools.partial(_scatter_rows_kernel, tm=tm)
    grid_spec = pltpu.PrefetchScalarGridSpec(
        num_scalar_prefetch=2,
        grid=(nt,),
        in_specs=[pl.BlockSpec((None, 1, tm), lambda i, zs, zv: (i, 0, 0), memory_space=pltpu.SMEM),
                  pl.BlockSpec((tm, D_MODEL), lambda i, zs, zv: (i, 0))],
        out_specs=pl.BlockSpec(memory_space=pl.ANY),
        scratch_shapes=[pltpu.VMEM((2, tm * ROW_SUB, LANES), F32), pltpu.VMEM((MOE_TILE * ROW_SUB, LANES), F32),
                        pltpu.SemaphoreType.DMA((2,)), pltpu.SemaphoreType.DMA],
    )
    return pl.pallas_call(
        kern,
        grid_spec=grid_spec,
        out_shape=jax.ShapeDtypeStruct((n_sorted * ROW_SUB, LANES), F32),
        compiler_params=pltpu.CompilerParams(
            dimension_semantics=("arbitrary",), vmem_limit_bytes=VMEM_LIMIT),
        name="moe_scatter_rows",
    )(zstart, zvalid, pos.reshape(nt, 1, tm), h)


def _expert_kernel(blk_ref, ea_ref, eb_ref, nvalid_ref,
                   x_ref, rwa_ref, rwb_ref, wga_ref, wua_ref, wda_ref, wgb_ref, wub_ref, wdb_ref, o_ref):
    @pl.when(pl.program_id(0) < nvalid_ref[0])
    def _():
        x = _tiles_to_rows(x_ref, MOE_TILE)
        xb = x.astype(BF16)
        aff_a = 1.0 / (1.0 + jnp.exp(-jnp.sum(x * rwa_ref[...], axis=-1, keepdims=True)))
        aff_b = 1.0 / (1.0 + jnp.exp(-jnp.sum(x * rwb_ref[...], axis=-1, keepdims=True)))
        tot = aff_a + aff_b
        h_a = (_silu(_dot(xb, wga_ref[...])) * _dot(xb, wua_ref[...]) * (aff_a / tot)).astype(BF16)
        h_b = (_silu(_dot(xb, wgb_ref[...])) * _dot(xb, wub_ref[...]) * (aff_b / tot)).astype(BF16)
        _rows_to_tiles(o_ref, _dot(h_a, wda_ref[...]) + _dot(h_b, wdb_ref[...]), MOE_TILE)

    @pl.when(pl.program_id(0) >= nvalid_ref[0])
    def _():
        o_ref[...] = jnp.zeros_like(o_ref)


def _expert_mlp(x_sorted, tile_blk, tile_ea, tile_eb, nvalid, rw_rows, wg, wu, wd):
    n_sorted = x_sorted.shape[0] // ROW_SUB
    nt = n_sorted // MOE_TILE
    xmap = lambda i, blk, ea, eb, nv: (blk[i], 0)
    amap = lambda i, blk, ea, eb, nv: (ea[i], 0, 0)
    bmap = lambda i, blk, ea, eb, nv: (eb[i], 0, 0)
    w_in = (None, D_MODEL, D_FF_EXPERT)
    w_out = (None, D_FF_EXPERT, D_MODEL)
    grid_spec = pltpu.PrefetchScalarGridSpec(
        num_scalar_prefetch=4,
        grid=(nt,),
        in_specs=[pl.BlockSpec((MOE_TILE * ROW_SUB, LANES), xmap),
                  pl.BlockSpec((None, 1, D_MODEL), amap), pl.BlockSpec((None, 1, D_MODEL), bmap),
                  pl.BlockSpec(w_in, amap), pl.BlockSpec(w_in, amap), pl.BlockSpec(w_out, amap),
                  pl.BlockSpec(w_in, bmap), pl.BlockSpec(w_in, bmap), pl.BlockSpec(w_out, bmap)],
        out_specs=pl.BlockSpec((MOE_TILE * ROW_SUB, LANES), lambda i, blk, ea, eb, nv: (i, 0)),
    )
    return pl.pallas_call(
        _expert_kernel,
        grid_spec=grid_spec,
        out_shape=jax.ShapeDtypeStruct((n_sorted * ROW_SUB, LANES), F32),
        compiler_params=pltpu.CompilerParams(
            dimension_semantics=("arbitrary",), vmem_limit_bytes=VMEM_LIMIT),
        name="moe_expert_mlp",
    )(tile_blk, tile_ea, tile_eb, nvalid, x_sorted, rw_rows, rw_rows, wg, wu, wd, wg, wu, wd)


def _gather_ln_kernel(pos_ref, pos_next_ref, f_ref, h_ref, g_ref, b_ref, o_ref, buf_ref, sems, *, tm):
    i = pl.program_id(0)
    slot = i % 2

    def fetch(p_ref, s):
        def issue(r2, carry):
            for pri in range(2):
                r = 2 * r2 + pri
                pltpu.make_async_copy(_row_tile(f_ref, p_ref[0, r]), _row_tile(buf_ref.at[s], r),
                                      sems.at[s]).start(priority=pri)
            return carry
        lax.fori_loop(0, tm // 2, issue, 0)

    @pl.when(i == 0)
    def _():
        fetch(pos_ref, slot)

    @pl.when(i + 1 < pl.num_programs(0))
    def _():
        fetch(pos_next_ref, 1 - slot)

    pltpu.make_async_copy(f_ref.at[pl.ds(0, tm * ROW_SUB)], buf_ref.at[slot], sems.at[slot]).wait()
    f = _tiles_to_rows(buf_ref.at[slot], tm)
    o_ref[...] = _layer_norm(DEEPNORM_ALPHA * h_ref[...] + f, g_ref[...], b_ref[...])


def _gather_ln(f_sorted, pos, h, g, b):
    n = h.shape[0]
    tm = min(ROW_TILE, n)
    nt = n // tm
    kern = functools.partial(_gather_ln_kernel, tm=tm)
    pos3 = pos.reshape(nt, 1, tm)
    return pl.pallas_call(
        kern,
        grid=(nt,),
        in_specs=[pl.BlockSpec((None, 1, tm), lambda i: (i, 0, 0), memory_space=pltpu.SMEM),
                  pl.BlockSpec((None, 1, tm), lambda i: (jnp.minimum(i + 1, nt - 1), 0, 0), memory_space=pltpu.SMEM),
                  pl.BlockSpec(memory_space=pl.ANY),
                  pl.BlockSpec((tm, D_MODEL), lambda i: (i, 0)),
                  _resident(g.shape), _resident(b.shape)],
        out_specs=pl.BlockSpec((tm, D_MODEL), lambda i: (i, 0)),
        out_shape=jax.ShapeDtypeStruct((n, D_MODEL), F32),
        scratch_shapes=[pltpu.VMEM((2, tm * ROW_SUB, LANES), F32), pltpu.SemaphoreType.DMA((2,))],
        compiler_params=pltpu.CompilerParams(
            dimension_semantics=("arbitrary",), vmem_limit_bytes=VMEM_LIMIT),
        name="moe_gather_deepnorm_ln",
    )(pos3, pos3, f_sorted, h, g, b)


def _moe_layer(h, rw_t, rb_col, rw_rows, wg, wu, wd, ln_g, ln_b):
    n = h.shape[0]
    n_tiles = n // MOE_TILE + N_BUCKETS
    n_sorted = n_tiles * MOE_TILE
    bucket, rank, counts = _router(h, rw_t, rb_col)
    tiles_per = (counts + MOE_TILE - 1) // MOE_TILE
    tile_end = jnp.cumsum(tiles_per)
    tile_start = tile_end - tiles_per
    pos = tile_start[bucket] * MOE_TILE + rank
    nvalid = tile_end[-1]
    tail = nvalid + jnp.arange(N_BUCKETS, dtype=I32)
    zstart = jnp.concatenate([jnp.maximum(tile_end - 1, 0), jnp.minimum(tail, n_tiles - 1)]) * MOE_TILE
    zvalid = jnp.concatenate([tiles_per > 0, tail < n_tiles]).astype(I32)
    tidx = jnp.minimum(jnp.arange(n_tiles, dtype=I32), nvalid - 1)
    tb = jnp.sum(tile_end[None, :] <= tidx[:, None], axis=1).astype(I32)
    pair_lo = jnp.array([p[0] for p in _PAIRS], I32)
    pair_hi = jnp.array([p[1] for p in _PAIRS], I32)
    tile_ea = (tb // N_PAIRS) * EXPERTS_PER_GROUP + pair_lo[tb % N_PAIRS]
    tile_eb = (tb // N_PAIRS) * EXPERTS_PER_GROUP + pair_hi[tb % N_PAIRS]
    x_sorted = _scatter_rows(h, pos.astype(I32), zstart.astype(I32), zvalid, n_sorted)
    f_sorted = _expert_mlp(x_sorted, tidx, tile_ea.astype(I32), tile_eb.astype(I32),
                           nvalid.reshape(1).astype(I32), rw_rows, wg, wu, wd)
    return _gather_ln(f_sorted, pos.astype(I32), h, ln_g, ln_b)


def _rot_half_slot(v, lane):
    w = v.shape[1]
    fwd = pltpu.roll(v, MLA_ROPE // 2, 1)
    bwd = pltpu.roll(v, w - MLA_ROPE // 2, 1)
    first = (lane >= MLA_NOPE) & (lane < MLA_NOPE + MLA_ROPE // 2)
    second = (lane >= MLA_NOPE + MLA_ROPE // 2) & (lane < MLA_NOPE + MLA_ROPE)
    return jnp.where(first, -bwd, jnp.where(second, fwd, 0.0))


def _mla_proj_kernel(x_ref, wd_ref, wkr_ref, qn_ref, kvn_ref, wuqt_ref, wuk_ref, wuvt_ref,
                     cos_ref, sin_ref, cost_ref, sint_ref, qt_ref, k_ref, vt_ref, *, tm):
    xb = x_ref[...].astype(BF16)
    down = _dot(xb, wd_ref[...])
    c_q = down[:, 0:MLA_Q_RANK]
    c_kv = down[:, MLA_Q_RANK:MLA_Q_RANK + MLA_KV_RANK]
    c_q = c_q * lax.rsqrt(jnp.mean(c_q * c_q, axis=-1, keepdims=True) + RMS_EPS) * qn_ref[...]
    c_kv = c_kv * lax.rsqrt(jnp.mean(c_kv * c_kv, axis=-1, keepdims=True) + RMS_EPS) * kvn_ref[...]
    c_q = c_q.astype(BF16)
    c_kv = c_kv.astype(BF16)

    lane = lax.broadcasted_iota(I32, (tm, HEAD_SLOT), 1)
    kr = _dot(xb, wkr_ref[...])
    kr = kr * cos_ref[...] + _rot_half_slot(kr, lane) * sin_ref[...]
    for h in range(MLA_N_HEADS):
        sl = slice(h * HEAD_SLOT, (h + 1) * HEAD_SLOT)
        k_ref[:, sl] = (_dot(c_kv, wuk_ref[:, sl]) + kr).astype(BF16)

    heads_per_dot = 4
    for hb in range(MLA_N_HEADS // heads_per_dot):
        rows = heads_per_dot * MLA_V
        vt = _dot_nt(wuvt_ref[hb * rows:(hb + 1) * rows, :], c_kv)
        for r in range(heads_per_dot):
            vt_ref[hb * heads_per_dot + r] = vt[r * MLA_V:(r + 1) * MLA_V].astype(BF16)

    cos_t = cost_ref[...]
    sin_t = sint_ref[...]
    half = MLA_ROPE // 2
    qscale = math.log2(math.e) / math.sqrt(MLA_NOPE + MLA_ROPE)
    for hb in range(MLA_N_HEADS // heads_per_dot):
        rows = heads_per_dot * HEAD_SLOT
        qt = _dot_nt(wuqt_ref[hb * rows:(hb + 1) * rows, :], c_q)
        for r in range(heads_per_dot):
            blk = qt[r * HEAD_SLOT:(r + 1) * HEAD_SLOT]
            t1 = blk[MLA_NOPE:MLA_NOPE + half]
            t2 = blk[MLA_NOPE + half:MLA_NOPE + MLA_ROPE]
            roped = jnp.concatenate([blk[0:MLA_NOPE], t1 * cos_t - t2 * sin_t, t1 * sin_t + t2 * cos_t,
                                     blk[MLA_NOPE + MLA_ROPE:HEAD_SLOT]], axis=0)
            qt_ref[hb * heads_per_dot + r] = (roped * qscale).astype(BF16)


def _mla_proj(h2, wd, wkr, qn, kvn, wuqt, wuk, wuvt, cos_s, sin_s, cos_t, sin_t, *, bsz, seq):
    n = bsz * seq
    tm = min(ROW_TILE, seq)
    nt = seq // tm
    row = lambda b, j: (b * nt + j, 0)
    kern = functools.partial(_mla_proj_kernel, tm=tm)
    return pl.pallas_call(
        kern,
        grid=(bsz, nt),
        in_specs=[pl.BlockSpec((tm, D_MODEL), row), _resident(wd.shape), _resident(wkr.shape),
                  _resident(qn.shape), _resident(kvn.shape), _resident(wuqt.shape), _resident(wuk.shape),
                  _resident(wuvt.shape),
                  pl.BlockSpec((tm, HEAD_SLOT), lambda b, j: (j, 0)), pl.BlockSpec((tm, HEAD_SLOT), lambda b, j: (j, 0)),
                  pl.BlockSpec((MLA_ROPE // 2, tm), lambda b, j: (0, j)),
                  pl.BlockSpec((MLA_ROPE // 2, tm), lambda b, j: (0, j))],
        out_specs=[pl.BlockSpec((None, MLA_N_HEADS, HEAD_SLOT, tm), lambda b, j: (b, 0, 0, j)),
                   pl.BlockSpec((tm, MLA_N_HEADS * HEAD_SLOT), row),
                   pl.BlockSpec((None, MLA_N_HEADS, None, MLA_V, tm), lambda b, j: (b, 0, j, 0, 0))],
        out_shape=[jax.ShapeDtypeStruct((bsz, MLA_N_HEADS, HEAD_SLOT, seq), BF16),
                   jax.ShapeDtypeStruct((n, MLA_N_HEADS * HEAD_SLOT), BF16),
                   jax.ShapeDtypeStruct((bsz, MLA_N_HEADS, nt, MLA_V, tm), BF16)],
        compiler_params=pltpu.CompilerParams(
            dimension_semantics=("arbitrary", "arbitrary"), vmem_limit_bytes=VMEM_LIMIT),
        name="mla_proj_rope",
    )(h2, wd, wkr, qn, kvn, wuqt, wuk, wuvt, cos_s, sin_s, cos_t, sin_t)


def _col_reduce(s, op):
    rows, w = s.shape
    r = op(s.reshape(4, rows // 32, 8, w), axis=1)
    return op(op(r, axis=0), axis=0, keepdims=True)


def _attn_kernel(qt_ref, k_ref, vt_ref, ot_ref, s0_ref, s1_ref, m_ref, l_ref, acc_ref, *, tq):
    qi = pl.program_id(2)
    tk = tq
    w = tq // 2
    q_halves = (qt_ref[:, 0:w], qt_ref[:, w:tq])

    def scores_into(s_ref, j):
        k_tile = k_ref[pl.ds(pl.multiple_of(j * tk, tk), tk), :]
        for hf in range(2):
            s_ref[hf] = _dot(k_tile, q_halves[hf])

    def update(s_ref, j, diagonal):
        vt = vt_ref[j]
        for hf in range(2):
            s = s_ref[hf]
            if diagonal:
                kc = lax.broadcasted_iota(I32, (tk, w), 0) // ATTN_CHUNK
                qc = lax.broadcasted_iota(I32, (tk, w), 1) // ATTN_CHUNK + hf * (w // ATTN_CHUNK)
                s = jnp.where(kc <= qc, s, -jnp.inf)
            m = m_ref[hf]
            m_new = jnp.maximum(m, _col_reduce(s, jnp.max))
            p = jnp.exp2(s - m_new)
            corr = jnp.exp2(m - m_new)
            l_ref[hf] = corr * l_ref[hf] + _col_reduce(p, jnp.sum)
            acc_ref[hf] = corr * acc_ref[hf] + _dot(vt, p.astype(BF16))
            m_ref[hf] = m_new

    m_ref[...] = jnp.full_like(m_ref, -jnp.inf)
    l_ref[...] = jnp.zeros_like(l_ref)
    acc_ref[...] = jnp.zeros_like(acc_ref)
    scores_into(s0_ref, 0)

    def pair(t, carry):
        scores_into(s1_ref, 2 * t + 1)
        update(s0_ref, 2 * t, False)
        scores_into(s0_ref, 2 * t + 2)
        update(s1_ref, 2 * t + 1, False)
        return carry

    lax.fori_loop(0, qi // 2, pair, 0)

    @pl.when(qi % 2 == 0)
    def _():
        update(s0_ref, qi, True)

    @pl.when(qi % 2 == 1)
    def _():
        scores_into(s1_ref, qi)
        update(s0_ref, qi - 1, False)
        update(s1_ref, qi, True)

    for hf in range(2):
        ot_ref[:, hf * w:(hf + 1) * w] = (acc_ref[hf] / l_ref[hf]).astype(BF16)


def _attention(qt, k, vt, *, bsz, seq):
    tq = min(ROW_TILE, seq)
    nq = seq // tq
    kern = functools.partial(_attn_kernel, tq=tq)
    return pl.pallas_call(
        kern,
        grid=(bsz, MLA_N_HEADS, nq),
        in_specs=[pl.BlockSpec((None, None, HEAD_SLOT, tq), lambda b, h, i: (b, h, 0, i)),
                  pl.BlockSpec((seq, HEAD_SLOT), lambda b, h, i: (b, h)),
                  pl.BlockSpec((None, None, nq, MLA_V, tq), lambda b, h, i: (b, h, 0, 0, 0))],
        out_specs=pl.BlockSpec((None, MLA_V, tq), lambda b, h, i: (b, h, i)),
        out_shape=jax.ShapeDtypeStruct((bsz, MLA_N_HEADS * MLA_V, seq), BF16),
        scratch_shapes=[pltpu.VMEM((2, tq, tq // 2), F32), pltpu.VMEM((2, tq, tq // 2), F32),
                        pltpu.VMEM((2, 1, tq // 2), F32), pltpu.VMEM((2, 1, tq // 2), F32),
                        pltpu.VMEM((2, MLA_V, tq // 2), F32)],
        compiler_params=pltpu.CompilerParams(
            dimension_semantics=("arbitrary", "arbitrary", "arbitrary"), vmem_limit_bytes=VMEM_LIMIT),
        name="mla_flash_attention",
    )(qt, k, vt)


def _proj_ln_t_kernel(at_ref, w_ref, res_ref, g_ref, b_ref, o_ref):
    m = _dot_tn(at_ref[...], w_ref[...])
    o_ref[...] = _layer_norm(DEEPNORM_ALPHA * res_ref[...] + m, g_ref[...], b_ref[...])


def _proj_ln_t(a_t, w, res, g, b, *, bsz, seq):
    k = a_t.shape[1]
    tm = min(ROW_TILE, seq)
    nt = seq // tm
    row = lambda bb, j: (bb * nt + j, 0)
    return pl.pallas_call(
        _proj_ln_t_kernel,
        grid=(bsz, nt),
        in_specs=[pl.BlockSpec((None, k, tm), lambda bb, j: (bb, 0, j)), _resident(w.shape),
                  pl.BlockSpec((tm, D_MODEL), row), _resident(g.shape), _resident(b.shape)],
        out_specs=pl.BlockSpec((tm, D_MODEL), row),
        out_shape=jax.ShapeDtypeStruct((bsz * seq, D_MODEL), F32),
        compiler_params=pltpu.CompilerParams(
            dimension_semantics=("arbitrary", "arbitrary"), vmem_limit_bytes=VMEM_LIMIT),
        name="outproj_t_deepnorm_ln",
    )(a_t, w, res, g, b)


def _head_slots(w, head_dim, n_heads):
    k = w.shape[0]
    w3 = w.reshape(k, n_heads, head_dim)
    return jnp.pad(w3, ((0, 0), (0, 0), (0, HEAD_SLOT - head_dim))).reshape(k, n_heads * HEAD_SLOT)


def kernel(x, ssd_w_in, ssd_conv_w, ssd_conv_b, ssd_dt_bias, ssd_a_log, ssd_d, ssd_norm_w, ssd_w_out, mla_w_down, mla_q_norm, mla_w_uq, mla_kv_norm, mla_w_ukv, mla_w_out, router_w, router_bias, moe_w_gate, moe_w_up, moe_w_down, ln_mix_g, ln_mix_b, ln_ffn_g, ln_ffn_b):
    bsz, seq, _ = x.shape
    n = bsz * seq
    h = x.reshape(n, D_MODEL)

    rw_t = router_w.T
    rw_rows = rw_t.reshape(N_EXPERTS, 1, D_MODEL)
    rb_col = router_bias.reshape(N_EXPERTS, 1)
    head_of_lane_p = jnp.arange(SSD_D_INNER) // SSD_HEAD_DIM
    e64 = (jnp.arange(LANES)[:, None] == head_of_lane_p[None, :]).astype(BF16)
    e128 = (jnp.arange(LANES)[:, None] == (jnp.arange(SSD_N_HEADS * SSD_L) // SSD_L)[None, :]).astype(BF16)

    for i in range(DEPTH):
        j = i // N_MIXERS
        if i % N_MIXERS == 0:
            w_in = ssd_w_in[j]
            wz = w_in[:, :SSD_D_INNER].astype(BF16)
            wx = w_in[:, SSD_D_INNER:SSD_D_INNER + SSD_CONV_DIM].astype(BF16)
            pad_h = LANES - SSD_N_HEADS
            wdt = jnp.pad(w_in[:, SSD_D_INNER + SSD_CONV_DIM:], ((0, 0), (0, pad_h))).astype(BF16)
            dt_bias = jnp.pad(ssd_dt_bias[j], (0, pad_h)).reshape(1, LANES)
            a_row = jnp.pad(-jnp.exp(ssd_a_log[j]), (0, pad_h)).reshape(1, LANES)
            d_exp = ssd_d[j][head_of_lane_p].reshape(1, SSD_D_INNER)
            z, xbc, dt = _ssd_inproj(h, wz, wx, wdt, ssd_conv_w[j], ssd_conv_b[j].reshape(1, SSD_CONV_DIM),
                                     dt_bias, bsz=bsz, seq=seq)
            y = _ssd_scan(xbc, z, dt, a_row, d_exp, ssd_norm_w[j].reshape(1, SSD_D_INNER), e128, e64,
                          bsz=bsz, seq=seq)
            h = _proj_ln(y, ssd_w_out[j].astype(BF16), h, ln_mix_g[i].reshape(1, D_MODEL),
                         ln_mix_b[i].reshape(1, D_MODEL))
        else:
            w_down = mla_w_down[j]
            wd = w_down[:, :MLA_Q_RANK + MLA_KV_RANK].astype(BF16)
            wkr = jnp.pad(w_down[:, MLA_Q_RANK + MLA_KV_RANK:],
                          ((0, 0), (MLA_NOPE, HEAD_SLOT - MLA_NOPE - MLA_ROPE))).astype(BF16)
            wuqt = _head_slots(mla_w_uq[j], MLA_NOPE + MLA_ROPE, MLA_N_HEADS).T.astype(BF16)
            wukv = mla_w_ukv[j].reshape(MLA_KV_RANK, MLA_N_HEADS, MLA_NOPE + MLA_V)
            wuk = _head_slots(wukv[:, :, :MLA_NOPE].reshape(MLA_KV_RANK, -1), MLA_NOPE, MLA_N_HEADS).astype(BF16)
            wuvt = wukv[:, :, MLA_NOPE:].reshape(MLA_KV_RANK, -1).T.astype(BF16)
            inv = ROPE_THETA ** (-jnp.arange(0, MLA_ROPE, 2, dtype=F32) / MLA_ROPE)
            ang = jnp.arange(seq, dtype=F32)[:, None] * inv[None, :]
            ones = jnp.ones((seq, MLA_NOPE), F32)
            zeros = jnp.zeros((seq, HEAD_SLOT - MLA_NOPE - MLA_ROPE), F32)
            cos_s = jnp.concatenate([ones, jnp.cos(ang), jnp.cos(ang), zeros], axis=1)
            sin_s = jnp.concatenate([0.0 * ones, jnp.sin(ang), jnp.sin(ang), zeros], axis=1)
            qt, k, vt = _mla_proj(h, wd, wkr, mla_q_norm[j].reshape(1, MLA_Q_RANK),
                                  mla_kv_norm[j].reshape(1, MLA_KV_RANK), wuqt, wuk, wuvt,
                                  cos_s, sin_s, jnp.cos(ang).T, jnp.sin(ang).T, bsz=bsz, seq=seq)
            y_t = _attention(qt, k, vt, bsz=bsz, seq=seq)
            h = _proj_ln_t(y_t, mla_w_out[j].astype(BF16), h, ln_mix_g[i].reshape(1, D_MODEL),
                           ln_mix_b[i].reshape(1, D_MODEL), bsz=bsz, seq=seq)
        h = _moe_layer(h, rw_t, rb_col, rw_rows, moe_w_gate[i].astype(BF16), moe_w_up[i].astype(BF16),
                       moe_w_down[i].astype(BF16), ln_ffn_g[i].reshape(1, D_MODEL), ln_ffn_b[i].reshape(1, D_MODEL))
    return h.reshape(bsz, seq, D_MODEL)
```

```python
import functools
import math

import jax
import jax.numpy as jnp
from jax import lax
from jax.experimental import pallas as pl
from jax.experimental.pallas import tpu as pltpu

F32 = jnp.float32
BF16 = jnp.bfloat16
I32 = jnp.int32

D_MODEL = 1024
DEPTH = 2
N_MIXERS = 2

SSD_D_INNER = 2048
SSD_HEAD_DIM = 64
SSD_N_HEADS = 32
SSD_N_GROUPS = 8
SSD_HEADS_PER_GROUP = 4
SSD_D_STATE = 128
SSD_CONV_W = 4
SSD_BC_DIM = SSD_N_GROUPS * SSD_D_STATE
SSD_CONV_DIM = SSD_D_INNER + 2 * SSD_BC_DIM

MLA_N_HEADS = 16
MLA_Q_RANK = 384
MLA_KV_RANK = 256
MLA_NOPE = 64
MLA_ROPE = 32
MLA_V = 64
ROPE_THETA = 10000.0
ATTN_CHUNK = 64

N_EXPERTS = 16
N_EXPERT_GROUPS = 4
EXPERTS_PER_GROUP = 4
D_FF_EXPERT = 512
N_PAIRS = 6
N_BUCKETS = N_EXPERT_GROUPS * N_PAIRS
BUCKET_ROWS = 32

DEEPNORM_ALPHA = (2.0 * DEPTH) ** 0.25
LN_EPS = 1e-5
RMS_EPS = 1e-6

LANES = 128
ROW_SUB = D_MODEL // LANES
HEAD_SLOT = 128
SSD_L = 128
VMEM_LIMIT = 56 * 1024 * 1024

ROW_TILE = 512
MOE_TILE = 256


def _silu(v):
    return v / (1.0 + jnp.exp2(v * (-math.log2(math.e))))


def _softplus(v):
    return jnp.maximum(v, 0.0) + jnp.log1p(jnp.exp(-jnp.abs(v)))


def _dot(a, b):
    return jnp.dot(a, b, preferred_element_type=F32)


def _dot_nt(a, b):
    return lax.dot_general(a, b, (((1,), (1,)), ((), ())), preferred_element_type=F32)


def _dot_tn(a, b):
    return lax.dot_general(a, b, (((0,), (0,)), ((), ())), preferred_element_type=F32)


def _split_bf16(v, terms):
    parts = []
    rem = v
    for _ in range(terms):
        p = rem.astype(BF16)
        parts.append(p)
        rem = rem - p.astype(F32)
    return parts


def _resident(shape):
    nd = len(shape)
    return pl.BlockSpec(shape, lambda *_: (0,) * nd)


def _inproj_kernel(x_ref, wz_ref, wx_ref, wdt_ref, cw_ref, cb_ref, dtb_ref,
                   z_ref, xbc_ref, dt_ref, buf_ref, carry_ref, *, tm, cn):
    @pl.when(pl.program_id(1) == 0)
    def _():
        carry_ref[...] = jnp.zeros_like(carry_ref)

    xb = x_ref[...].astype(BF16)
    for c in range(SSD_D_INNER // cn):
        sl = slice(c * cn, (c + 1) * cn)
        z_ref[:, sl] = _dot(xb, wz_ref[:, sl]).astype(BF16)
    dt_ref[...] = _softplus(_dot(xb, wdt_ref[...]) + dtb_ref[...])
    for c in range(SSD_CONV_DIM // cn):
        sl = slice(c * cn, (c + 1) * cn)
        r = _dot(xb, wx_ref[:, sl])
        buf_ref[0:8, :] = carry_ref[:, sl]
        buf_ref[8:8 + tm, :] = r
        carry_ref[:, sl] = r[tm - 8:tm, :]
        w = cw_ref[:, sl]
        taps = [jnp.broadcast_to(w[k:k + 1, :], (8, cn)) for k in range(SSD_CONV_W)]
        bias = jnp.broadcast_to(cb_ref[:, sl], (8, cn))
        view = lambda v: v.reshape(tm // 8, 8, cn)
        y = (bias
             + taps[3] * view(r)
             + taps[2] * view(buf_ref[7:7 + tm, :])
             + taps[1] * view(buf_ref[6:6 + tm, :])
             + taps[0] * view(buf_ref[5:5 + tm, :]))
        xbc_ref[:, sl] = _silu(y).reshape(tm, cn).astype(BF16)


def _ssd_inproj(x2, wz, wx, wdt, conv_w, conv_b, dt_bias, *, bsz, seq):
    n = bsz * seq
    tm = min(ROW_TILE, seq)
    cn = 512
    nt = seq // tm
    row = lambda b, j: (b * nt + j, 0)
    kern = functools.partial(_inproj_kernel, tm=tm, cn=cn)
    return pl.pallas_call(
        kern,
        grid=(bsz, nt),
        in_specs=[
            pl.BlockSpec((tm, D_MODEL), row),
            _resident(wz.shape), _resident(wx.shape), _resident(wdt.shape),
            _resident(conv_w.shape), _resident(conv_b.shape), _resident(dt_bias.shape),
        ],
        out_specs=[
            pl.BlockSpec((tm, SSD_D_INNER), row),
            pl.BlockSpec((tm, SSD_CONV_DIM), row),
            pl.BlockSpec((tm, LANES), row),
        ],
        out_shape=[
            jax.ShapeDtypeStruct((n, SSD_D_INNER), BF16),
            jax.ShapeDtypeStruct((n, SSD_CONV_DIM), BF16),
            jax.ShapeDtypeStruct((n, LANES), F32),
        ],
        scratch_shapes=[pltpu.VMEM((8 + tm, cn), F32), pltpu.VMEM((8, SSD_CONV_DIM), F32)],
        compiler_params=pltpu.CompilerParams(
            dimension_semantics=("arbitrary", "arbitrary"), vmem_limit_bytes=VMEM_LIMIT),
        name="ssd_inproj_conv",
    )(x2, wz, wx, wdt, conv_w, conv_b, dt_bias)


def _ssd_scan_kernel(xbc_ref, z_ref, dt_ref, arow_ref, dexp_ref, nw_ref, e128_ref, e64_ref,
                     y_ref, state_ref):
    L = SSD_L
    G, R, P, NS = SSD_N_GROUPS, SSD_HEADS_PER_GROUP, SSD_HEAD_DIM, SSD_D_STATE

    @pl.when(pl.program_id(1) == 0)
    def _():
        state_ref[...] = jnp.zeros_like(state_ref)

    row_i = lax.broadcasted_iota(I32, (L, L), 0)
    col_i = lax.broadcasted_iota(I32, (L, L), 1)
    tril = jnp.where(col_i <= row_i, 1.0, 0.0).astype(BF16)
    neg = jnp.where(col_i <= row_i, 0.0, -jnp.inf)

    dt = dt_ref[...]
    a = dt * arow_ref[...]
    a_parts = _split_bf16(a, 3)
    acum = _dot(tril, a_parts[0]) + _dot(tril, a_parts[1]) + _dot(tril, a_parts[2])

    ac_parts = _split_bf16(acum, 2)
    acum2 = ac_parts[0].astype(F32) + ac_parts[1].astype(F32)
    acum_rows = acum2.T
    ex_t = _dot(jnp.concatenate(ac_parts, axis=0), e128_ref[...])
    acum_t = ex_t[0:L] + ex_t[L:2 * L]
    lane = lax.broadcasted_iota(I32, (L, LANES), 1)
    lo_half = lane < P
    acum_p = jnp.concatenate(
        [jnp.where(lo_half, acum_t[:, (2 * k) * L:(2 * k + 1) * L], acum_t[:, (2 * k + 1) * L:(2 * k + 2) * L])
         for k in range(SSD_N_HEADS // 2)], axis=1)
    dt_p = _dot(dt.astype(BF16), e64_ref[...])

    x = xbc_ref[:, 0:SSD_D_INNER].astype(F32)
    last_p = acum_p[L - 1:L, :]
    xdt = (x * dt_p).astype(BF16)
    xw = (x * (jnp.exp2(last_p - acum_p) * dt_p)).astype(BF16)
    exp_a_p = jnp.exp2(acum_p)
    exp_last_p = jnp.exp2(last_p)

    y_parts = []
    for g in range(G):
        b_g = xbc_ref[:, SSD_D_INNER + g * NS:SSD_D_INNER + (g + 1) * NS]
        c_g = xbc_ref[:, SSD_D_INNER + SSD_BC_DIM + g * NS:SSD_D_INNER + SSD_BC_DIM + (g + 1) * NS]
        scores = _dot_nt(c_g, b_g)
        gsl = slice(g * R * P, (g + 1) * R * P)
        y_off = _dot(c_g, state_ref[g].astype(BF16)) * exp_a_p[:, gsl]
        y_diag = []
        for pr in range(R // 2):
            h0 = g * R + 2 * pr
            m_pair = []
            for h in (h0, h0 + 1):
                seg = acum_t[:, h * L:(h + 1) * L] - acum_rows[h:h + 1, :]
                m_pair.append((scores * jnp.exp2(seg + neg)).astype(BF16))
            xs = xdt[:, h0 * P:(h0 + 2) * P]
            zero = jnp.zeros_like(xs)
            bd = jnp.concatenate([jnp.where(lo_half, xs, zero), jnp.where(lo_half, zero, xs)], axis=0)
            y_diag.append(_dot(jnp.concatenate(m_pair, axis=1), bd))
        y_parts.append(jnp.concatenate(y_diag, axis=1) + y_off)
        state_ref[g] = state_ref[g] * exp_last_p[:, gsl] + _dot_tn(b_g, xw[:, gsl])

    y = jnp.concatenate(y_parts, axis=1) + x * dexp_ref[...]
    yz = y * _silu(z_ref[...].astype(F32))
    outs = []
    for g in range(G):
        blk = yz[:, g * R * P:(g + 1) * R * P]
        ms = jnp.mean(blk * blk, axis=-1, keepdims=True)
        outs.append(blk * lax.rsqrt(ms + RMS_EPS))
    y_ref[...] = (jnp.concatenate(outs, axis=1) * nw_ref[...]).astype(BF16)


def _ssd_scan(xbc, z, dt, a_row, d_exp, norm_w, e128, e64, *, bsz, seq):
    n = bsz * seq
    L = SSD_L
    nt = seq // L
    row = lambda b, j: (b * nt + j, 0)
    return pl.pallas_call(
        _ssd_scan_kernel,
        grid=(bsz, nt),
        in_specs=[
            pl.BlockSpec((L, SSD_CONV_DIM), row),
            pl.BlockSpec((L, SSD_D_INNER), row),
            pl.BlockSpec((L, LANES), row),
            _resident(a_row.shape), _resident(d_exp.shape), _resident(norm_w.shape),
            _resident(e128.shape), _resident(e64.shape),
        ],
        out_specs=pl.BlockSpec((L, SSD_D_INNER), row),
        out_shape=jax.ShapeDtypeStruct((n, SSD_D_INNER), BF16),
        scratch_shapes=[pltpu.VMEM((SSD_N_GROUPS, SSD_D_STATE, SSD_HEADS_PER_GROUP * SSD_HEAD_DIM), F32)],
        compiler_params=pltpu.CompilerParams(
            dimension_semantics=("arbitrary", "arbitrary"), vmem_limit_bytes=VMEM_LIMIT),
        name="ssd_scan_gate_norm",
    )(xbc, z, dt, a_row, d_exp, norm_w, e128, e64)


def _layer_norm(v, g, b):
    mu = jnp.mean(v, axis=-1, keepdims=True)
    c = v - mu
    var = jnp.mean(c * c, axis=-1, keepdims=True)
    return c * lax.rsqrt(var + LN_EPS) * g + b


def _proj_ln_kernel(a_ref, w_ref, res_ref, g_ref, b_ref, o_ref):
    m = _dot(a_ref[...], w_ref[...])
    o_ref[...] = _layer_norm(DEEPNORM_ALPHA * res_ref[...] + m, g_ref[...], b_ref[...])


def _proj_ln(a, w, res, g, b):
    n, k = a.shape
    tm = min(ROW_TILE, n)
    row = lambda i: (i, 0)
    return pl.pallas_call(
        _proj_ln_kernel,
        grid=(n // tm,),
        in_specs=[pl.BlockSpec((tm, k), row), _resident(w.shape), pl.BlockSpec((tm, D_MODEL), row),
                  _resident(g.shape), _resident(b.shape)],
        out_specs=pl.BlockSpec((tm, D_MODEL), row),
        out_shape=jax.ShapeDtypeStruct((n, D_MODEL), F32),
        compiler_params=pltpu.CompilerParams(
            dimension_semantics=("arbitrary",), vmem_limit_bytes=VMEM_LIMIT),
        name="outproj_deepnorm_ln",
    )(a, w, res, g, b)


_PAIRS = ((0, 1), (0, 2), (0, 3), (1, 2), (1, 3), (2, 3))


def _router_kernel(h_ref, rwt_ref, rb_ref, bucket_ref, rank_ref, counts_ref, base_ref, *, tm):
    @pl.when(pl.program_id(0) == 0)
    def _():
        base_ref[...] = jnp.zeros_like(base_ref)

    wp = _split_bf16(rwt_ref[...], 2)
    hp = _split_bf16(h_ref[...], 2)
    logits = (_dot_nt(wp[1], hp[0]) + _dot_nt(wp[0], hp[1])) + _dot_nt(wp[0], hp[0])
    sel = 1.0 / (1.0 + jnp.exp(-logits)) + rb_ref[...]
    rows = [sel[e:e + 1, :] for e in range(N_EXPERTS)]

    best_g = jnp.zeros((1, tm), I32)
    best_s = None
    for g in range(N_EXPERT_GROUPS):
        v = rows[4 * g:4 * g + 4]
        gs = None
        for (i, j) in _PAIRS:
            s = v[i] + v[j]
            gs = s if gs is None else jnp.maximum(gs, s)
        if best_s is None:
            best_s = gs
        else:
            better = gs > best_s
            best_g = jnp.where(better, g, best_g)
            best_s = jnp.where(better, gs, best_s)

    v = []
    for i in range(EXPERTS_PER_GROUP):
        acc = rows[i]
        for g in range(1, N_EXPERT_GROUPS):
            acc = jnp.where(best_g == g, rows[4 * g + i], acc)
        v.append(acc)
    keep = []
    for i in range(EXPERTS_PER_GROUP):
        beaten = jnp.zeros((1, tm), I32)
        for j in range(EXPERTS_PER_GROUP):
            if j == i:
                continue
            wins = (v[j] > v[i]) if j > i else (v[j] >= v[i])
            beaten = beaten + jnp.where(wins, 1, 0)
        keep.append(jnp.where(beaten < 2, 1, 0))
    pair = jnp.full((1, tm), N_PAIRS - 1, I32)
    for p in range(N_PAIRS - 2, -1, -1):
        i, j = _PAIRS[p]
        pair = jnp.where(keep[i] * keep[j] > 0, p, pair)
    bucket = best_g * N_PAIRS + pair

    brow = lax.broadcasted_iota(I32, (BUCKET_ROWS, tm), 0)
    onehot = jnp.where(brow == bucket, 1.0, 0.0)
    ti = lax.broadcasted_iota(I32, (tm, tm), 0)
    tj = lax.broadcasted_iota(I32, (tm, tm), 1)
    before = jnp.where(ti < tj, 1.0, 0.0).astype(BF16)
    prefix = _dot(onehot.astype(BF16), before)
    base = base_ref[:, 0:1]
    rank = jnp.sum(onehot * (prefix + base), axis=0, keepdims=True)
    cnt = jnp.sum(onehot, axis=1, keepdims=True)
    base_ref[...] = base_ref[...] + cnt
    bucket_ref[...] = bucket
    rank_ref[...] = rank.astype(I32)
    counts_ref[...] = base_ref[...].astype(I32)


def _router(h, rw_t, rb_col):
    n = h.shape[0]
    tm = min(ROW_TILE, n)
    nt = n // tm
    kern = functools.partial(_router_kernel, tm=tm)
    bucket, rank, counts = pl.pallas_call(
        kern,
        grid=(nt,),
        in_specs=[pl.BlockSpec((tm, D_MODEL), lambda i: (i, 0)), _resident(rw_t.shape), _resident(rb_col.shape)],
        out_specs=[pl.BlockSpec((None, 1, tm), lambda i: (i, 0, 0)),
                   pl.BlockSpec((None, 1, tm), lambda i: (i, 0, 0)),
                   pl.BlockSpec((BUCKET_ROWS, LANES), lambda i: (0, 0))],
        out_shape=[jax.ShapeDtypeStruct((nt, 1, tm), I32), jax.ShapeDtypeStruct((nt, 1, tm), I32),
                   jax.ShapeDtypeStruct((BUCKET_ROWS, LANES), I32)],
        scratch_shapes=[pltpu.VMEM((BUCKET_ROWS, LANES), F32)],
        compiler_params=pltpu.CompilerParams(
            dimension_semantics=("arbitrary",), vmem_limit_bytes=VMEM_LIMIT),
        name="moe_router",
    )(h, rw_t, rb_col)
    return bucket.reshape(n), rank.reshape(n), counts[:N_BUCKETS, 0]


def _rows_to_tiles(ref, v, rows):
    for j in range(ROW_SUB):
        ref[pl.ds(j, rows, stride=ROW_SUB), :] = v[:, j * LANES:(j + 1) * LANES]


def _tiles_to_rows(ref, rows):
    return jnp.concatenate([ref[pl.ds(j, rows, stride=ROW_SUB), :] for j in range(ROW_SUB)], axis=1)


def _row_tile(ref, r):
    return ref.at[pl.ds(pl.multiple_of(r * ROW_SUB, ROW_SUB), ROW_SUB)]


def _scatter_rows_kernel(zstart_ref, zvalid_ref, pos_ref, h_ref, o_ref, buf_ref, zeros_ref, sems, zsem, *, tm):
    def zero_copy(b):
        start = pl.multiple_of(zstart_ref[b] * ROW_SUB, ROW_SUB)
        return pltpu.make_async_copy(zeros_ref, o_ref.at[pl.ds(start, MOE_TILE * ROW_SUB)], zsem)

    @pl.when(pl.program_id(0) == 0)
    def _():
        zeros_ref[...] = jnp.zeros_like(zeros_ref)
        for b in range(2 * N_BUCKETS):
            pl.when(zvalid_ref[b] > 0)(lambda b=b: zero_copy(b).start())
        for b in range(2 * N_BUCKETS):
            pl.when(zvalid_ref[b] > 0)(lambda b=b: zero_copy(b).wait())

    i = pl.program_id(0)
    last = pl.num_programs(0) - 1
    slot = i % 2
    buf = buf_ref.at[slot]

    def wait_rows(s):
        pltpu.make_async_copy(buf_ref.at[s], o_ref.at[pl.ds(0, tm * ROW_SUB)], sems.at[s]).wait()

    @pl.when(i >= 2)
    def _():
        wait_rows(slot)

    _rows_to_tiles(buf, h_ref[...], tm)

    def issue(r2, carry):
        for pri in range(2):
            r = 2 * r2 + pri
            pltpu.make_async_copy(_row_tile(buf, r), _row_tile(o_ref, pos_ref[0, r]),
                                  sems.at[slot]).start(priority=pri)
        return carry

    lax.fori_loop(0, tm // 2, issue, 0)

    @pl.when(i == last)
    def _():
        wait_rows(slot)

        @pl.when(last >= 1)
        def _():
            wait_rows(1 - slot)


def _scatter_rows(h, pos, zstart, zvalid, n_sorted):
    n = h.shape[0]
    tm = min(ROW_TILE, n)
    nt = n // tm
    kern = functools.partial(_scatter_rows_kernel, tm=tm)
    grid_spec = pltpu.PrefetchScalarGridSpec(
        num_scalar_prefetch=2,
        grid=(nt,),
        in_specs=[pl.BlockSpec((None, 1, tm), lambda i, zs, zv: (i, 0, 0), memory_space=pltpu.SMEM),
                  pl.BlockSpec((tm, D_MODEL), lambda i, zs, zv: (i, 0))],
        out_specs=pl.BlockSpec(memory_space=pl.ANY),
        scratch_shapes=[pltpu.VMEM((2, tm * ROW_SUB, LANES), F32), pltpu.VMEM((MOE_TILE * ROW_SUB, LANES), F32),
                        pltpu.SemaphoreType.DMA((2,)), pltpu.SemaphoreType.DMA],
    )
    return pl.pallas_call(
        kern,
        grid_spec=grid_spec,
        out_shape=jax.ShapeDtypeStruct((n_sorted * ROW_SUB, LANES), F32),
        compiler_params=pltpu.CompilerParams(
            dimension_semantics=("arbitrary",), vmem_limit_bytes=VMEM_LIMIT),
        name="moe_scatter_rows",
    )(zstart, zvalid, pos.reshape(nt, 1, tm), h)


def _expert_kernel(blk_ref, ea_ref, eb_ref, nvalid_ref,
                   x_ref, rwa_ref, rwb_ref, wga_ref, wua_ref, wda_ref, wgb_ref, wub_ref, wdb_ref, o_ref,
                   wa_in, wa_out, wb_in, wb_out):
    i = pl.program_id(0)
    prev = jnp.maximum(i - 1, 0)

    @pl.when((i == 0) | (ea_ref[i] != ea_ref[prev]))
    def _():
        wa_in[0] = wga_ref[...].astype(BF16)
        wa_in[1] = wua_ref[...].astype(BF16)
        wa_out[...] = wda_ref[...].astype(BF16)

    @pl.when((i == 0) | (eb_ref[i] != eb_ref[prev]))
    def _():
        wb_in[0] = wgb_ref[...].astype(BF16)
        wb_in[1] = wub_ref[...].astype(BF16)
        wb_out[...] = wdb_ref[...].astype(BF16)

    @pl.when(i < nvalid_ref[0])
    def _():
        x = _tiles_to_rows(x_ref, MOE_TILE)
        xb = x.astype(BF16)
        aff_a = 1.0 / (1.0 + jnp.exp(-jnp.sum(x * rwa_ref[...], axis=-1, keepdims=True)))
        aff_b = 1.0 / (1.0 + jnp.exp(-jnp.sum(x * rwb_ref[...], axis=-1, keepdims=True)))
        tot = aff_a + aff_b
        h_a = (_silu(_dot(xb, wa_in[0])) * _dot(xb, wa_in[1]) * (aff_a / tot)).astype(BF16)
        h_b = (_silu(_dot(xb, wb_in[0])) * _dot(xb, wb_in[1]) * (aff_b / tot)).astype(BF16)
        _rows_to_tiles(o_ref, _dot(h_a, wa_out[...]) + _dot(h_b, wb_out[...]), MOE_TILE)

    @pl.when(pl.program_id(0) >= nvalid_ref[0])
    def _():
        o_ref[...] = jnp.zeros_like(o_ref)


def _expert_mlp(x_sorted, tile_blk, tile_ea, tile_eb, nvalid, rw_rows, wg, wu, wd, layer):
    n_sorted = x_sorted.shape[0] // ROW_SUB
    nt = n_sorted // MOE_TILE
    xmap = lambda i, blk, ea, eb, nv: (blk[i], 0)
    amap = lambda i, blk, ea, eb, nv: (ea[i], 0, 0)
    bmap = lambda i, blk, ea, eb, nv: (eb[i], 0, 0)
    wamap = lambda i, blk, ea, eb, nv: (layer, ea[i], 0, 0)
    wbmap = lambda i, blk, ea, eb, nv: (layer, eb[i], 0, 0)
    w_in = (None, None, D_MODEL, D_FF_EXPERT)
    w_out = (None, None, D_FF_EXPERT, D_MODEL)
    grid_spec = pltpu.PrefetchScalarGridSpec(
        num_scalar_prefetch=4,
        grid=(nt,),
        in_specs=[pl.BlockSpec((MOE_TILE * ROW_SUB, LANES), xmap),
                  pl.BlockSpec((None, 1, D_MODEL), amap), pl.BlockSpec((None, 1, D_MODEL), bmap),
                  pl.BlockSpec(w_in, wamap), pl.BlockSpec(w_in, wamap), pl.BlockSpec(w_out, wamap),
                  pl.BlockSpec(w_in, wbmap), pl.BlockSpec(w_in, wbmap), pl.BlockSpec(w_out, wbmap)],
        out_specs=pl.BlockSpec((MOE_TILE * ROW_SUB, LANES), lambda i, blk, ea, eb, nv: (i, 0)),
        scratch_shapes=[pltpu.VMEM((2, D_MODEL, D_FF_EXPERT), BF16), pltpu.VMEM((D_FF_EXPERT, D_MODEL), BF16),
                        pltpu.VMEM((2, D_MODEL, D_FF_EXPERT), BF16), pltpu.VMEM((D_FF_EXPERT, D_MODEL), BF16)],
    )
    return pl.pallas_call(
        _expert_kernel,
        grid_spec=grid_spec,
        out_shape=jax.ShapeDtypeStruct((n_sorted * ROW_SUB, LANES), F32),
        compiler_params=pltpu.CompilerParams(
            dimension_semantics=("arbitrary",), vmem_limit_bytes=VMEM_LIMIT),
        name="moe_expert_mlp",
    )(tile_blk, tile_ea, tile_eb, nvalid, x_sorted, rw_rows, rw_rows, wg, wu, wd, wg, wu, wd)


def _gather_ln_kernel(pos_ref, pos_next_ref, f_ref, h_ref, g_ref, b_ref, o_ref, buf_ref, sems, *, tm):
    i = pl.program_id(0)
    slot = i % 2

    def fetch(p_ref, s):
        def issue(r2, carry):
            for pri in range(2):
                r = 2 * r2 + pri
                pltpu.make_async_copy(_row_tile(f_ref, p_ref[0, r]), _row_tile(buf_ref.at[s], r),
                                      sems.at[s]).start(priority=pri)
            return carry
        lax.fori_loop(0, tm // 2, issue, 0)

    @pl.when(i == 0)
    def _():
        fetch(pos_ref, slot)

    @pl.when(i + 1 < pl.num_programs(0))
    def _():
        fetch(pos_next_ref, 1 - slot)

    pltpu.make_async_copy(f_ref.at[pl.ds(0, tm * ROW_SUB)], buf_ref.at[slot], sems.at[slot]).wait()
    f = _tiles_to_rows(buf_ref.at[slot], tm)
    o_ref[...] = _layer_norm(DEEPNORM_ALPHA * h_ref[...] + f, g_ref[...], b_ref[...])


def _gather_ln(f_sorted, pos, h, g, b):
    n = h.shape[0]
    tm = min(ROW_TILE, n)
    nt = n // tm
    kern = functools.partial(_gather_ln_kernel, tm=tm)
    pos3 = pos.reshape(nt, 1, tm)
    return pl.pallas_call(
        kern,
        grid=(nt,),
        in_specs=[pl.BlockSpec((None, 1, tm), lambda i: (i, 0, 0), memory_space=pltpu.SMEM),
                  pl.BlockSpec((None, 1, tm), lambda i: (jnp.minimum(i + 1, nt - 1), 0, 0), memory_space=pltpu.SMEM),
                  pl.BlockSpec(memory_space=pl.ANY),
                  pl.BlockSpec((tm, D_MODEL), lambda i: (i, 0)),
                  _resident(g.shape), _resident(b.shape)],
        out_specs=pl.BlockSpec((tm, D_MODEL), lambda i: (i, 0)),
        out_shape=jax.ShapeDtypeStruct((n, D_MODEL), F32),
        scratch_shapes=[pltpu.VMEM((2, tm * ROW_SUB, LANES), F32), pltpu.SemaphoreType.DMA((2,))],
        compiler_params=pltpu.CompilerParams(
            dimension_semantics=("arbitrary",), vmem_limit_bytes=VMEM_LIMIT),
        name="moe_gather_deepnorm_ln",
    )(pos3, pos3, f_sorted, h, g, b)


def _moe_layer(h, rw_t, rb_col, rw_rows, wg, wu, wd, layer, ln_g, ln_b):
    n = h.shape[0]
    n_tiles = n // MOE_TILE + N_BUCKETS
    n_sorted = n_tiles * MOE_TILE
    bucket, rank, counts = _router(h, rw_t, rb_col)
    tiles_per = (counts + MOE_TILE - 1) // MOE_TILE
    tile_end = jnp.cumsum(tiles_per)
    tile_start = tile_end - tiles_per
    pos = tile_start[bucket] * MOE_TILE + rank
    nvalid = tile_end[-1]
    tail = nvalid + jnp.arange(N_BUCKETS, dtype=I32)
    zstart = jnp.concatenate([jnp.maximum(tile_end - 1, 0), jnp.minimum(tail, n_tiles - 1)]) * MOE_TILE
    zvalid = jnp.concatenate([tiles_per > 0, tail < n_tiles]).astype(I32)
    tidx = jnp.minimum(jnp.arange(n_tiles, dtype=I32), nvalid - 1)
    tb = jnp.sum(tile_end[None, :] <= tidx[:, None], axis=1).astype(I32)
    pair_lo = jnp.array([p[0] for p in _PAIRS], I32)
    pair_hi = jnp.array([p[1] for p in _PAIRS], I32)
    tile_ea = (tb // N_PAIRS) * EXPERTS_PER_GROUP + pair_lo[tb % N_PAIRS]
    tile_eb = (tb // N_PAIRS) * EXPERTS_PER_GROUP + pair_hi[tb % N_PAIRS]
    x_sorted = _scatter_rows(h, pos.astype(I32), zstart.astype(I32), zvalid, n_sorted)
    f_sorted = _expert_mlp(x_sorted, tidx, tile_ea.astype(I32), tile_eb.astype(I32),
                           nvalid.reshape(1).astype(I32), rw_rows, wg, wu, wd, layer)
    return _gather_ln(f_sorted, pos.astype(I32), h, ln_g, ln_b)


def _rot_half_slot(v, lane):
    w = v.shape[1]
    fwd = pltpu.roll(v, MLA_ROPE // 2, 1)
    bwd = pltpu.roll(v, w - MLA_ROPE // 2, 1)
    first = (lane >= MLA_NOPE) & (lane < MLA_NOPE + MLA_ROPE // 2)
    second = (lane >= MLA_NOPE + MLA_ROPE // 2) & (lane < MLA_NOPE + MLA_ROPE)
    return jnp.where(first, -bwd, jnp.where(second, fwd, 0.0))


def _mla_proj_kernel(x_ref, wd_ref, wkr_ref, qn_ref, kvn_ref, wuqt_ref, wuk_ref, wuvt_ref,
                     cos_ref, sin_ref, cost_ref, sint_ref, qt_ref, k_ref, vt_ref, *, tm):
    xb = x_ref[...].astype(BF16)
    down = _dot(xb, wd_ref[...])
    c_q = down[:, 0:MLA_Q_RANK]
    c_kv = down[:, MLA_Q_RANK:MLA_Q_RANK + MLA_KV_RANK]
    c_q = c_q * lax.rsqrt(jnp.mean(c_q * c_q, axis=-1, keepdims=True) + RMS_EPS) * qn_ref[...]
    c_kv = c_kv * lax.rsqrt(jnp.mean(c_kv * c_kv, axis=-1, keepdims=True) + RMS_EPS) * kvn_ref[...]
    c_q = c_q.astype(BF16)
    c_kv = c_kv.astype(BF16)

    lane = lax.broadcasted_iota(I32, (tm, HEAD_SLOT), 1)
    kr = _dot(xb, wkr_ref[...])
    kr = kr * cos_ref[...] + _rot_half_slot(kr, lane) * sin_ref[...]
    for h in range(MLA_N_HEADS):
        sl = slice(h * HEAD_SLOT, (h + 1) * HEAD_SLOT)
        k_ref[:, sl] = (_dot(c_kv, wuk_ref[:, sl]) + kr).astype(BF16)

    heads_per_dot = 4
    for hb in range(MLA_N_HEADS // heads_per_dot):
        rows = heads_per_dot * MLA_V
        vt = _dot_nt(wuvt_ref[hb * rows:(hb + 1) * rows, :], c_kv)
        for r in range(heads_per_dot):
            vt_ref[hb * heads_per_dot + r] = vt[r * MLA_V:(r + 1) * MLA_V].astype(BF16)

    cos_t = cost_ref[...]
    sin_t = sint_ref[...]
    half = MLA_ROPE // 2
    qscale = math.log2(math.e) / math.sqrt(MLA_NOPE + MLA_ROPE)
    for hb in range(MLA_N_HEADS // heads_per_dot):
        rows = heads_per_dot * HEAD_SLOT
        qt = _dot_nt(wuqt_ref[hb * rows:(hb + 1) * rows, :], c_q)
        for r in range(heads_per_dot):
            blk = qt[r * HEAD_SLOT:(r + 1) * HEAD_SLOT]
            t1 = blk[MLA_NOPE:MLA_NOPE + half]
            t2 = blk[MLA_NOPE + half:MLA_NOPE + MLA_ROPE]
            roped = jnp.concatenate([blk[0:MLA_NOPE], t1 * cos_t - t2 * sin_t, t1 * sin_t + t2 * cos_t,
                                     blk[MLA_NOPE + MLA_ROPE:HEAD_SLOT]], axis=0)
            qt_ref[hb * heads_per_dot + r] = (roped * qscale).astype(BF16)


def _mla_proj(h2, wd, wkr, qn, kvn, wuqt, wuk, wuvt, cos_s, sin_s, cos_t, sin_t, *, bsz, seq):
    n = bsz * seq
    tm = min(ROW_TILE, seq)
    nt = seq // tm
    row = lambda b, j: (b * nt + j, 0)
    kern = functools.partial(_mla_proj_kernel, tm=tm)
    return pl.pallas_call(
        kern,
        grid=(bsz, nt),
        in_specs=[pl.BlockSpec((tm, D_MODEL), row), _resident(wd.shape), _resident(wkr.shape),
                  _resident(qn.shape), _resident(kvn.shape), _resident(wuqt.shape), _resident(wuk.shape),
                  _resident(wuvt.shape),
                  pl.BlockSpec((tm, HEAD_SLOT), lambda b, j: (j, 0)), pl.BlockSpec((tm, HEAD_SLOT), lambda b, j: (j, 0)),
                  pl.BlockSpec((MLA_ROPE // 2, tm), lambda b, j: (0, j)),
                  pl.BlockSpec((MLA_ROPE // 2, tm), lambda b, j: (0, j))],
        out_specs=[pl.BlockSpec((None, MLA_N_HEADS, HEAD_SLOT, tm), lambda b, j: (b, 0, 0, j)),
                   pl.BlockSpec((tm, MLA_N_HEADS * HEAD_SLOT), row),
                   pl.BlockSpec((None, MLA_N_HEADS, None, MLA_V, tm), lambda b, j: (b, 0, j, 0, 0))],
        out_shape=[jax.ShapeDtypeStruct((bsz, MLA_N_HEADS, HEAD_SLOT, seq), BF16),
                   jax.ShapeDtypeStruct((n, MLA_N_HEADS * HEAD_SLOT), BF16),
                   jax.ShapeDtypeStruct((bsz, MLA_N_HEADS, nt, MLA_V, tm), BF16)],
        compiler_params=pltpu.CompilerParams(
            dimension_semantics=("arbitrary", "arbitrary"), vmem_limit_bytes=VMEM_LIMIT),
        name="mla_proj_rope",
    )(h2, wd, wkr, qn, kvn, wuqt, wuk, wuvt, cos_s, sin_s, cos_t, sin_t)


def _col_reduce(s, op):
    rows, w = s.shape
    r = op(s.reshape(4, rows // 32, 8, w), axis=1)
    return op(op(r, axis=0), axis=0, keepdims=True)


def _attn_kernel(qt_ref, k_ref, vt_ref, ot_ref, s0_ref, s1_ref, m_ref, l_ref, acc_ref, *, tq):
    qi = pl.program_id(2)
    tk = tq
    w = tq // 2
    q_halves = (qt_ref[:, 0:w], qt_ref[:, w:tq])

    def scores_into(s_ref, j):
        k_tile = k_ref[pl.ds(pl.multiple_of(j * tk, tk), tk), :]
        for hf in range(2):
            s_ref[hf] = _dot(k_tile, q_halves[hf])

    def update(s_ref, j, diagonal):
        vt = vt_ref[j]
        for hf in range(2):
            s = s_ref[hf]
            if diagonal:
                kc = lax.broadcasted_iota(I32, (tk, w), 0) // ATTN_CHUNK
                qc = lax.broadcasted_iota(I32, (tk, w), 1) // ATTN_CHUNK + hf * (w // ATTN_CHUNK)
                s = jnp.where(kc <= qc, s, -jnp.inf)
            m = m_ref[hf]
            m_new = jnp.maximum(m, _col_reduce(s, jnp.max))
            p = jnp.exp2(s - m_new)
            corr = jnp.exp2(m - m_new)
            l_ref[hf] = corr * l_ref[hf] + _col_reduce(p, jnp.sum)
            acc_ref[hf] = corr * acc_ref[hf] + _dot(vt, p.astype(BF16))
            m_ref[hf] = m_new

    m_ref[...] = jnp.full_like(m_ref, -jnp.inf)
    l_ref[...] = jnp.zeros_like(l_ref)
    acc_ref[...] = jnp.zeros_like(acc_ref)
    scores_into(s0_ref, 0)

    def pair(t, carry):
        scores_into(s1_ref, 2 * t + 1)
        update(s0_ref, 2 * t, False)
        scores_into(s0_ref, 2 * t + 2)
        update(s1_ref, 2 * t + 1, False)
        return carry

    lax.fori_loop(0, qi // 2, pair, 0)

    @pl.when(qi % 2 == 0)
    def _():
        update(s0_ref, qi, True)

    @pl.when(qi % 2 == 1)
    def _():
        scores_into(s1_ref, qi)
        update(s0_ref, qi - 1, False)
        update(s1_ref, qi, True)

    for hf in range(2):
        ot_ref[:, hf * w:(hf + 1) * w] = (acc_ref[hf] / l_ref[hf]).astype(BF16)


def _attention(qt, k, vt, *, bsz, seq):
    tq = min(ROW_TILE, seq)
    nq = seq // tq
    kern = functools.partial(_attn_kernel, tq=tq)
    return pl.pallas_call(
        kern,
        grid=(bsz, MLA_N_HEADS, nq),
        in_specs=[pl.BlockSpec((None, None, HEAD_SLOT, tq), lambda b, h, i: (b, h, 0, i)),
                  pl.BlockSpec((seq, HEAD_SLOT), lambda b, h, i: (b, h)),
                  pl.BlockSpec((None, None, nq, MLA_V, tq), lambda b, h, i: (b, h, 0, 0, 0))],
        out_specs=pl.BlockSpec((None, MLA_V, tq), lambda b, h, i: (b, h, i)),
        out_shape=jax.ShapeDtypeStruct((bsz, MLA_N_HEADS * MLA_V, seq), BF16),
        scratch_shapes=[pltpu.VMEM((2, tq, tq // 2), F32), pltpu.VMEM((2, tq, tq // 2), F32),
                        pltpu.VMEM((2, 1, tq // 2), F32), pltpu.VMEM((2, 1, tq // 2), F32),
                        pltpu.VMEM((2, MLA_V, tq // 2), F32)],
        compiler_params=pltpu.CompilerParams(
            dimension_semantics=("arbitrary", "arbitrary", "arbitrary"), vmem_limit_bytes=VMEM_LIMIT),
        name="mla_flash_attention",
    )(qt, k, vt)


def _proj_ln_t_kernel(at_ref, w_ref, res_ref, g_ref, b_ref, o_ref):
    m = _dot_tn(at_ref[...], w_ref[...])
    o_ref[...] = _layer_norm(DEEPNORM_ALPHA * res_ref[...] + m, g_ref[...], b_ref[...])


def _proj_ln_t(a_t, w, res, g, b, *, bsz, seq):
    k = a_t.shape[1]
    tm = min(ROW_TILE, seq)
    nt = seq // tm
    row = lambda bb, j: (bb * nt + j, 0)
    return pl.pallas_call(
        _proj_ln_t_kernel,
        grid=(bsz, nt),
        in_specs=[pl.BlockSpec((None, k, tm), lambda bb, j: (bb, 0, j)), _resident(w.shape),
                  pl.BlockSpec((tm, D_MODEL), row), _resident(g.shape), _resident(b.shape)],
        out_specs=pl.BlockSpec((tm, D_MODEL), row),
        out_shape=jax.ShapeDtypeStruct((bsz * seq, D_MODEL), F32),
        compiler_params=pltpu.CompilerParams(
            dimension_semantics=("arbitrary", "arbitrary"), vmem_limit_bytes=VMEM_LIMIT),
        name="outproj_t_deepnorm_ln",
    )(a_t, w, res, g, b)


def _head_slots(w, head_dim, n_heads):
    k = w.shape[0]
    w3 = w.reshape(k, n_heads, head_dim)
    return jnp.pad(w3, ((0, 0), (0, 0), (0, HEAD_SLOT - head_dim))).reshape(k, n_heads * HEAD_SLOT)


def kernel(x, ssd_w_in, ssd_conv_w, ssd_conv_b, ssd_dt_bias, ssd_a_log, ssd_d, ssd_norm_w, ssd_w_out, mla_w_down, mla_q_norm, mla_w_uq, mla_kv_norm, mla_w_ukv, mla_w_out, router_w, router_bias, moe_w_gate, moe_w_up, moe_w_down, ln_mix_g, ln_mix_b, ln_ffn_g, ln_ffn_b):
    bsz, seq, _ = x.shape
    n = bsz * seq
    h = x.reshape(n, D_MODEL)

    rw_t = router_w.T
    rw_rows = rw_t.reshape(N_EXPERTS, 1, D_MODEL)
    rb_col = router_bias.reshape(N_EXPERTS, 1)
    head_of_lane_p = jnp.arange(SSD_D_INNER) // SSD_HEAD_DIM
    e64 = (jnp.arange(LANES)[:, None] == head_of_lane_p[None, :]).astype(BF16)
    e128 = (jnp.arange(LANES)[:, None] == (jnp.arange(SSD_N_HEADS * SSD_L) // SSD_L)[None, :]).astype(BF16)

    for i in range(DEPTH):
        j = i // N_MIXERS
        if i % N_MIXERS == 0:
            w_in = ssd_w_in[j]
            wz = w_in[:, :SSD_D_INNER].astype(BF16)
            wx = w_in[:, SSD_D_INNER:SSD_D_INNER + SSD_CONV_DIM].astype(BF16)
            pad_h = LANES - SSD_N_HEADS
            wdt = jnp.pad(w_in[:, SSD_D_INNER + SSD_CONV_DIM:], ((0, 0), (0, pad_h))).astype(BF16)
            dt_bias = jnp.pad(ssd_dt_bias[j], (0, pad_h)).reshape(1, LANES)
            a_row = jnp.pad(-jnp.exp(ssd_a_log[j]) * math.log2(math.e), (0, pad_h)).reshape(1, LANES)
            d_exp = ssd_d[j][head_of_lane_p].reshape(1, SSD_D_INNER)
            z, xbc, dt = _ssd_inproj(h, wz, wx, wdt, ssd_conv_w[j], ssd_conv_b[j].reshape(1, SSD_CONV_DIM),
                                     dt_bias, bsz=bsz, seq=seq)
            y = _ssd_scan(xbc, z, dt, a_row, d_exp, ssd_norm_w[j].reshape(1, SSD_D_INNER), e128, e64,
                          bsz=bsz, seq=seq)
            h = _proj_ln(y, ssd_w_out[j].astype(BF16), h, ln_mix_g[i].reshape(1, D_MODEL),
                         ln_mix_b[i].reshape(1, D_MODEL))
        else:
            w_down = mla_w_down[j]
            wd = w_down[:, :MLA_Q_RANK + MLA_KV_RANK].astype(BF16)
            wkr = jnp.pad(w_down[:, MLA_Q_RANK + MLA_KV_RANK:],
                          ((0, 0), (MLA_NOPE, HEAD_SLOT - MLA_NOPE - MLA_ROPE))).astype(BF16)
            wuqt = _head_slots(mla_w_uq[j], MLA_NOPE + MLA_ROPE, MLA_N_HEADS).T.astype(BF16)
            wukv = mla_w_ukv[j].reshape(MLA_KV_RANK, MLA_N_HEADS, MLA_NOPE + MLA_V)
            wuk = _head_slots(wukv[:, :, :MLA_NOPE].reshape(MLA_KV_RANK, -1), MLA_NOPE, MLA_N_HEADS).astype(BF16)
            wuvt = wukv[:, :, MLA_NOPE:].reshape(MLA_KV_RANK, -1).T.astype(BF16)
            inv = ROPE_THETA ** (-jnp.arange(0, MLA_ROPE, 2, dtype=F32) / MLA_ROPE)
            ang = jnp.arange(seq, dtype=F32)[:, None] * inv[None, :]
            ones = jnp.ones((seq, MLA_NOPE), F32)
            zeros = jnp.zeros((seq, HEAD_SLOT - MLA_NOPE - MLA_ROPE), F32)
            cos_s = jnp.concatenate([ones, jnp.cos(ang), jnp.cos(ang), zeros], axis=1)
            sin_s = jnp.concatenate([0.0 * ones, jnp.sin(ang), jnp.sin(ang), zeros], axis=1)
            qt, k, vt = _mla_proj(h, wd, wkr, mla_q_norm[j].reshape(1, MLA_Q_RANK),
                                  mla_kv_norm[j].reshape(1, MLA_KV_RANK), wuqt, wuk, wuvt,
                                  cos_s, sin_s, jnp.cos(ang).T, jnp.sin(ang).T, bsz=bsz, seq=seq)
            y_t = _attention(qt, k, vt, bsz=bsz, seq=seq)
            h = _proj_ln_t(y_t, mla_w_out[j].astype(BF16), h, ln_mix_g[i].reshape(1, D_MODEL),
                           ln_mix_b[i].reshape(1, D_MODEL), bsz=bsz, seq=seq)
        h = _moe_layer(h, rw_t, rb_col, rw_rows, moe_w_gate, moe_w_up, moe_w_down, i,
                       ln_ffn_g[i].reshape(1, D_MODEL), ln_ffn_b[i].reshape(1, D_MODEL))
    return h.reshape(bsz, seq, D_MODEL)
```

```python
import functools
import math

import jax
import jax.numpy as jnp
from jax import lax
from jax.experimental import pallas as pl
from jax.experimental.pallas import tpu as pltpu

F32 = jnp.float32
BF16 = jnp.bfloat16
I32 = jnp.int32

D_MODEL = 1024
DEPTH = 2
N_MIXERS = 2

SSD_D_INNER = 2048
SSD_HEAD_DIM = 64
SSD_N_HEADS = 32
SSD_N_GROUPS = 8
SSD_HEADS_PER_GROUP = 4
SSD_D_STATE = 128
SSD_CONV_W = 4
SSD_BC_DIM = SSD_N_GROUPS * SSD_D_STATE
SSD_CONV_DIM = SSD_D_INNER + 2 * SSD_BC_DIM

MLA_N_HEADS = 16
MLA_Q_RANK = 384
MLA_KV_RANK = 256
MLA_NOPE = 64
MLA_ROPE = 32
MLA_V = 64
ROPE_THETA = 10000.0
ATTN_CHUNK = 64
ATTN_W = 256
MASK_LANE0 = MLA_NOPE + MLA_ROPE
MASK_BIG = 2.0 ** 100

N_EXPERTS = 16
N_EXPERT_GROUPS = 4
EXPERTS_PER_GROUP = 4
D_FF_EXPERT = 512
N_PAIRS = 6
N_BUCKETS = N_EXPERT_GROUPS * N_PAIRS
BUCKET_ROWS = 32

DEEPNORM_ALPHA = (2.0 * DEPTH) ** 0.25
LN_EPS = 1e-5
RMS_EPS = 1e-6

LANES = 128
ROW_SUB = D_MODEL // LANES
HEAD_SLOT = 128
SSD_L = 128
VMEM_LIMIT = 56 * 1024 * 1024

ROW_TILE = 512
MOE_TILE = 256


def _silu(v):
    return v / (1.0 + jnp.exp2(v * (-math.log2(math.e))))


def _softplus(v):
    return jnp.maximum(v, 0.0) + jnp.log1p(jnp.exp(-jnp.abs(v)))


def _dot(a, b):
    return jnp.dot(a, b, preferred_element_type=F32)


def _dot_nt(a, b):
    return lax.dot_general(a, b, (((1,), (1,)), ((), ())), preferred_element_type=F32)


def _dot_tn(a, b):
    return lax.dot_general(a, b, (((0,), (0,)), ((), ())), preferred_element_type=F32)


def _split_bf16(v, terms):
    parts = []
    rem = v
    for _ in range(terms):
        p = rem.astype(BF16)
        parts.append(p)
        rem = rem - p.astype(F32)
    return parts


def _resident(shape):
    nd = len(shape)
    return pl.BlockSpec(shape, lambda *_: (0,) * nd)


def _inproj_kernel(x_ref, wz_ref, wx_ref, wdt_ref, cw_ref, cb_ref, dtb_ref,
                   z_ref, xbc_ref, dt_ref, buf_ref, carry_ref, *, tm, cn):
    @pl.when(pl.program_id(1) == 0)
    def _():
        carry_ref[...] = jnp.zeros_like(carry_ref)

    xb = x_ref[...].astype(BF16)
    for c in range(SSD_D_INNER // cn):
        sl = slice(c * cn, (c + 1) * cn)
        z_ref[:, sl] = _dot(xb, wz_ref[:, sl]).astype(BF16)
    dt_ref[...] = _softplus(_dot(xb, wdt_ref[...]) + dtb_ref[...])
    for c in range(SSD_CONV_DIM // cn):
        sl = slice(c * cn, (c + 1) * cn)
        r = _dot(xb, wx_ref[:, sl])
        buf_ref[0:8, :] = carry_ref[:, sl]
        buf_ref[8:8 + tm, :] = r
        carry_ref[:, sl] = r[tm - 8:tm, :]
        w = cw_ref[:, sl]
        taps = [jnp.broadcast_to(w[k:k + 1, :], (8, cn)) for k in range(SSD_CONV_W)]
        bias = jnp.broadcast_to(cb_ref[:, sl], (8, cn))
        view = lambda v: v.reshape(tm // 8, 8, cn)
        y = (bias
             + taps[3] * view(r)
             + taps[2] * view(buf_ref[7:7 + tm, :])
             + taps[1] * view(buf_ref[6:6 + tm, :])
             + taps[0] * view(buf_ref[5:5 + tm, :]))
        xbc_ref[:, sl] = _silu(y).reshape(tm, cn).astype(BF16)


def _ssd_inproj(x2, wz, wx, wdt, conv_w, conv_b, dt_bias, *, bsz, seq):
    n = bsz * seq
    tm = min(ROW_TILE, seq)
    cn = 512
    nt = seq // tm
    row = lambda b, j: (b * nt + j, 0)
    kern = functools.partial(_inproj_kernel, tm=tm, cn=cn)
    return pl.pallas_call(
        kern,
        grid=(bsz, nt),
        in_specs=[
            pl.BlockSpec((tm, D_MODEL), row),
            _resident(wz.shape), _resident(wx.shape), _resident(wdt.shape),
            _resident(conv_w.shape), _resident(conv_b.shape), _resident(dt_bias.shape),
        ],
        out_specs=[
            pl.BlockSpec((tm, SSD_D_INNER), row),
            pl.BlockSpec((tm, SSD_CONV_DIM), row),
            pl.BlockSpec((tm, LANES), row),
        ],
        out_shape=[
            jax.ShapeDtypeStruct((n, SSD_D_INNER), BF16),
            jax.ShapeDtypeStruct((n, SSD_CONV_DIM), BF16),
            jax.ShapeDtypeStruct((n, LANES), F32),
        ],
        scratch_shapes=[pltpu.VMEM((8 + tm, cn), F32), pltpu.VMEM((8, SSD_CONV_DIM), F32)],
        compiler_params=pltpu.CompilerParams(
            dimension_semantics=("arbitrary", "arbitrary"), vmem_limit_bytes=VMEM_LIMIT),
        name="ssd_inproj_conv",
    )(x2, wz, wx, wdt, conv_w, conv_b, dt_bias)


def _ssd_scan_kernel(xbc_ref, z_ref, dt_ref, arow_ref, dexp_ref, nw_ref, e128_ref, e64_ref,
                     y_ref, state_ref):
    L = SSD_L
    G, R, P, NS = SSD_N_GROUPS, SSD_HEADS_PER_GROUP, SSD_HEAD_DIM, SSD_D_STATE

    @pl.when(pl.program_id(1) == 0)
    def _():
        state_ref[...] = jnp.zeros_like(state_ref)

    row_i = lax.broadcasted_iota(I32, (L, L), 0)
    col_i = lax.broadcasted_iota(I32, (L, L), 1)
    tril = jnp.where(col_i <= row_i, 1.0, 0.0).astype(BF16)
    neg = jnp.where(col_i <= row_i, 0.0, -jnp.inf)

    dt = dt_ref[...]
    a = dt * arow_ref[...]
    a_parts = _split_bf16(a, 3)
    acum = _dot(tril, a_parts[0]) + _dot(tril, a_parts[1]) + _dot(tril, a_parts[2])

    ac_parts = _split_bf16(acum, 2)
    acum2 = ac_parts[0].astype(F32) + ac_parts[1].astype(F32)
    acum_rows = acum2.T
    ex_t = _dot(jnp.concatenate(ac_parts, axis=0), e128_ref[...])
    acum_t = ex_t[0:L] + ex_t[L:2 * L]
    lane = lax.broadcasted_iota(I32, (L, LANES), 1)
    lo_half = lane < P
    acum_p = jnp.concatenate(
        [jnp.where(lo_half, acum_t[:, (2 * k) * L:(2 * k + 1) * L], acum_t[:, (2 * k + 1) * L:(2 * k + 2) * L])
         for k in range(SSD_N_HEADS // 2)], axis=1)
    dt_p = _dot(dt.astype(BF16), e64_ref[...])

    x = xbc_ref[:, 0:SSD_D_INNER].astype(F32)
    last_p = acum_p[L - 1:L, :]
    xdt = (x * dt_p).astype(BF16)
    xw = (x * (jnp.exp2(last_p - acum_p) * dt_p)).astype(BF16)
    exp_a_p = jnp.exp2(acum_p)
    exp_last_p = jnp.exp2(last_p)

    y_parts = []
    for g in range(G):
        b_g = xbc_ref[:, SSD_D_INNER + g * NS:SSD_D_INNER + (g + 1) * NS]
        c_g = xbc_ref[:, SSD_D_INNER + SSD_BC_DIM + g * NS:SSD_D_INNER + SSD_BC_DIM + (g + 1) * NS]
        scores = _dot_nt(c_g, b_g)
        gsl = slice(g * R * P, (g + 1) * R * P)
        y_off = _dot(c_g, state_ref[g].astype(BF16)) * exp_a_p[:, gsl]
        y_diag = []
        for pr in range(R // 2):
            h0 = g * R + 2 * pr
            m_pair = []
            for h in (h0, h0 + 1):
                seg = acum_t[:, h * L:(h + 1) * L] - acum_rows[h:h + 1, :]
                m_pair.append((scores * jnp.exp2(seg + neg)).astype(BF16))
            xs = xdt[:, h0 * P:(h0 + 2) * P]
            zero = jnp.zeros_like(xs)
            bd = jnp.concatenate([jnp.where(lo_half, xs, zero), jnp.where(lo_half, zero, xs)], axis=0)
            y_diag.append(_dot(jnp.concatenate(m_pair, axis=1), bd))
        y_parts.append(jnp.concatenate(y_diag, axis=1) + y_off)
        state_ref[g] = state_ref[g] * exp_last_p[:, gsl] + _dot_tn(b_g, xw[:, gsl])

    y = jnp.concatenate(y_parts, axis=1) + x * dexp_ref[...]
    yz = y * _silu(z_ref[...].astype(F32))
    outs = []
    for g in range(G):
        blk = yz[:, g * R * P:(g + 1) * R * P]
        ms = jnp.mean(blk * blk, axis=-1, keepdims=True)
        outs.append(blk * lax.rsqrt(ms + RMS_EPS))
    y_ref[...] = (jnp.concatenate(outs, axis=1) * nw_ref[...]).astype(BF16)


def _ssd_scan(xbc, z, dt, a_row, d_exp, norm_w, e128, e64, *, bsz, seq):
    n = bsz * seq
    L = SSD_L
    nt = seq // L
    row = lambda b, j: (b * nt + j, 0)
    return pl.pallas_call(
        _ssd_scan_kernel,
        grid=(bsz, nt),
        in_specs=[
            pl.BlockSpec((L, SSD_CONV_DIM), row),
            pl.BlockSpec((L, SSD_D_INNER), row),
            pl.BlockSpec((L, LANES), row),
            _resident(a_row.shape), _resident(d_exp.shape), _resident(norm_w.shape),
            _resident(e128.shape), _resident(e64.shape),
        ],
        out_specs=pl.BlockSpec((L, SSD_D_INNER), row),
        out_shape=jax.ShapeDtypeStruct((n, SSD_D_INNER), BF16),
        scratch_shapes=[pltpu.VMEM((SSD_N_GROUPS, SSD_D_STATE, SSD_HEADS_PER_GROUP * SSD_HEAD_DIM), F32)],
        compiler_params=pltpu.CompilerParams(
            dimension_semantics=("arbitrary", "arbitrary"), vmem_limit_bytes=VMEM_LIMIT),
        name="ssd_scan_gate_norm",
    )(xbc, z, dt, a_row, d_exp, norm_w, e128, e64)


def _layer_norm(v, g, b):
    mu = jnp.mean(v, axis=-1, keepdims=True)
    c = v - mu
    var = jnp.mean(c * c, axis=-1, keepdims=True)
    return c * lax.rsqrt(var + LN_EPS) * g + b


def _proj_ln_kernel(a_ref, w_ref, res_ref, g_ref, b_ref, o_ref):
    m = _dot(a_ref[...], w_ref[...])
    o_ref[...] = _layer_norm(DEEPNORM_ALPHA * res_ref[...] + m, g_ref[...], b_ref[...])


def _proj_ln(a, w, res, g, b):
    n, k = a.shape
    tm = min(ROW_TILE, n)
    row = lambda i: (i, 0)
    return pl.pallas_call(
        _proj_ln_kernel,
        grid=(n // tm,),
        in_specs=[pl.BlockSpec((tm, k), row), _resident(w.shape), pl.BlockSpec((tm, D_MODEL), row),
                  _resident(g.shape), _resident(b.shape)],
        out_specs=pl.BlockSpec((tm, D_MODEL), row),
        out_shape=jax.ShapeDtypeStruct((n, D_MODEL), F32),
        compiler_params=pltpu.CompilerParams(
            dimension_semantics=("arbitrary",), vmem_limit_bytes=VMEM_LIMIT),
        name="outproj_deepnorm_ln",
    )(a, w, res, g, b)


_PAIRS = ((0, 1), (0, 2), (0, 3), (1, 2), (1, 3), (2, 3))


def _router_kernel(h_ref, rwt_ref, rb_ref, bucket_ref, rank_ref, counts_ref, base_ref, *, tm):
    @pl.when(pl.program_id(0) == 0)
    def _():
        base_ref[...] = jnp.zeros_like(base_ref)

    wp = _split_bf16(rwt_ref[...], 2)
    hp = _split_bf16(h_ref[...], 2)
    logits = (_dot_nt(wp[1], hp[0]) + _dot_nt(wp[0], hp[1])) + _dot_nt(wp[0], hp[0])
    sel = 1.0 / (1.0 + jnp.exp(-logits)) + rb_ref[...]
    rows = [sel[e:e + 1, :] for e in range(N_EXPERTS)]

    best_g = jnp.zeros((1, tm), I32)
    best_s = None
    for g in range(N_EXPERT_GROUPS):
        v = rows[4 * g:4 * g + 4]
        gs = None
        for (i, j) in _PAIRS:
            s = v[i] + v[j]
            gs = s if gs is None else jnp.maximum(gs, s)
        if best_s is None:
            best_s = gs
        else:
            better = gs > best_s
            best_g = jnp.where(better, g, best_g)
            best_s = jnp.where(better, gs, best_s)

    v = []
    for i in range(EXPERTS_PER_GROUP):
        acc = rows[i]
        for g in range(1, N_EXPERT_GROUPS):
            acc = jnp.where(best_g == g, rows[4 * g + i], acc)
        v.append(acc)
    keep = []
    for i in range(EXPERTS_PER_GROUP):
        beaten = jnp.zeros((1, tm), I32)
        for j in range(EXPERTS_PER_GROUP):
            if j == i:
                continue
            wins = (v[j] > v[i]) if j > i else (v[j] >= v[i])
            beaten = beaten + jnp.where(wins, 1, 0)
        keep.append(jnp.where(beaten < 2, 1, 0))
    pair = jnp.full((1, tm), N_PAIRS - 1, I32)
    for p in range(N_PAIRS - 2, -1, -1):
        i, j = _PAIRS[p]
        pair = jnp.where(keep[i] * keep[j] > 0, p, pair)
    bucket = best_g * N_PAIRS + pair

    brow = lax.broadcasted_iota(I32, (BUCKET_ROWS, tm), 0)
    onehot = jnp.where(brow == bucket, 1.0, 0.0)
    ti = lax.broadcasted_iota(I32, (tm, tm), 0)
    tj = lax.broadcasted_iota(I32, (tm, tm), 1)
    before = jnp.where(ti < tj, 1.0, 0.0).astype(BF16)
    prefix = _dot(onehot.astype(BF16), before)
    base = base_ref[:, 0:1]
    rank = jnp.sum(onehot * (prefix + base), axis=0, keepdims=True)
    cnt = jnp.sum(onehot, axis=1, keepdims=True)
    base_ref[...] = base_ref[...] + cnt
    bucket_ref[...] = bucket
    rank_ref[...] = rank.astype(I32)
    counts_ref[...] = base_ref[...].astype(I32)


def _router(h, rw_t, rb_col):
    n = h.shape[0]
    tm = min(ROW_TILE, n)
    nt = n // tm
    kern = functools.partial(_router_kernel, tm=tm)
    bucket, rank, counts = pl.pallas_call(
        kern,
        grid=(nt,),
        in_specs=[pl.BlockSpec((tm, D_MODEL), lambda i: (i, 0)), _resident(rw_t.shape), _resident(rb_col.shape)],
        out_specs=[pl.BlockSpec((None, 1, tm), lambda i: (i, 0, 0)),
                   pl.BlockSpec((None, 1, tm), lambda i: (i, 0, 0)),
                   pl.BlockSpec((BUCKET_ROWS, LANES), lambda i: (0, 0))],
        out_shape=[jax.ShapeDtypeStruct((nt, 1, tm), I32), jax.ShapeDtypeStruct((nt, 1, tm), I32),
                   jax.ShapeDtypeStruct((BUCKET_ROWS, LANES), I32)],
        scratch_shapes=[pltpu.VMEM((BUCKET_ROWS, LANES), F32)],
        compiler_params=pltpu.CompilerParams(
            dimension_semantics=("arbitrary",), vmem_limit_bytes=VMEM_LIMIT),
        name="moe_router",
    )(h, rw_t, rb_col)
    return bucket.reshape(n), rank.reshape(n), counts[:N_BUCKETS, 0]


def _rows_to_tiles(ref, v, rows):
    for j in range(ROW_SUB):
        ref[pl.ds(j, rows, stride=ROW_SUB), :] = v[:, j * LANES:(j + 1) * LANES]


def _tiles_to_rows(ref, rows):
    return jnp.concatenate([ref[pl.ds(j, rows, stride=ROW_SUB), :] for j in range(ROW_SUB)], axis=1)


def _row_tile(ref, r):
    return ref.at[pl.ds(pl.multiple_of(r * ROW_SUB, ROW_SUB), ROW_SUB)]


def _scatter_rows_kernel(zstart_ref, zvalid_ref, pos_ref, h_ref, o_ref, buf_ref, zeros_ref, sems, zsem, *, tm):
    def zero_copy(b):
        start = pl.multiple_of(zstart_ref[b] * ROW_SUB, ROW_SUB)
        return pltpu.make_async_copy(zeros_ref, o_ref.at[pl.ds(start, MOE_TILE * ROW_SUB)], zsem)

    @pl.when(pl.program_id(0) == 0)
    def _():
        zeros_ref[...] = jnp.zeros_like(zeros_ref)
        for b in range(2 * N_BUCKETS):
            pl.when(zvalid_ref[b] > 0)(lambda b=b: zero_copy(b).start())
        for b in range(2 * N_BUCKETS):
            pl.when(zvalid_ref[b] > 0)(lambda b=b: zero_copy(b).wait())

    i = pl.program_id(0)
    last = pl.num_programs(0) - 1
    slot = i % 2
    buf = buf_ref.at[slot]

    def wait_rows(s):
        pltpu.make_async_copy(buf_ref.at[s], o_ref.at[pl.ds(0, tm * ROW_SUB)], sems.at[s]).wait()

    @pl.when(i >= 2)
    def _():
        wait_rows(slot)

    _rows_to_tiles(buf, h_ref[...], tm)

    def issue(r2, carry):
        for pri in range(2):
            r = 2 * r2 + pri
            pltpu.make_async_copy(_row_tile(buf, r), _row_tile(o_ref, pos_ref[0, r]),
                                  sems.at[slot]).start(priority=pri)
        return carry

    lax.fori_loop(0, tm // 2, issue, 0)

    @pl.when(i == last)
    def _():
        wait_rows(slot)

        @pl.when(last >= 1)
        def _():
            wait_rows(1 - slot)


def _scatter_rows(h, pos, zstart, zvalid, n_sorted):
    n = h.shape[0]
    tm = min(ROW_TILE, n)
    nt = n // tm
    kern = functools.partial(_scatter_rows_kernel, tm=tm)
    grid_spec = pltpu.PrefetchScalarGridSpec(
        num_scalar_prefetch=2,
        grid=(nt,),
        in_specs=[pl.BlockSpec((None, 1, tm), lambda i, zs, zv: (i, 0, 0), memory_space=pltpu.SMEM),
                  pl.BlockSpec((tm, D_MODEL), lambda i, zs, zv: (i, 0))],
        out_specs=pl.BlockSpec(memory_space=pl.ANY),
        scratch_shapes=[pltpu.VMEM((2, tm * ROW_SUB, LANES), F32), pltpu.VMEM((MOE_TILE * ROW_SUB, LANES), F32),
                        pltpu.SemaphoreType.DMA((2,)), pltpu.SemaphoreType.DMA],
    )
    return pl.pallas_call(
        kern,
        grid_spec=grid_spec,
        out_shape=jax.ShapeDtypeStruct((n_sorted * ROW_SUB, LANES), F32),
        compiler_params=pltpu.CompilerParams(
            dimension_semantics=("arbitrary",), vmem_limit_bytes=VMEM_LIMIT),
        name="moe_scatter_rows",
    )(zstart, zvalid, pos.reshape(nt, 1, tm), h)


def _expert_kernel(blk_ref, ea_ref, eb_ref, nvalid_ref,
                   x_ref, rwa_ref, rwb_ref, wga_ref, wua_ref, wda_ref, wgb_ref, wub_ref, wdb_ref, o_ref,
                   wa_in, wa_out, wb_in, wb_out):
    i = pl.program_id(0)
    prev = jnp.maximum(i - 1, 0)

    @pl.when((i == 0) | (ea_ref[i] != ea_ref[prev]))
    def _():
        wa_in[0] = wga_ref[...].astype(BF16)
        wa_in[1] = wua_ref[...].astype(BF16)
        wa_out[...] = wda_ref[...].astype(BF16)

    @pl.when((i == 0) | (eb_ref[i] != eb_ref[prev]))
    def _():
        wb_in[0] = wgb_ref[...].astype(BF16)
        wb_in[1] = wub_ref[...].astype(BF16)
        wb_out[...] = wdb_ref[...].astype(BF16)

    @pl.when(i < nvalid_ref[0])
    def _():
        x = _tiles_to_rows(x_ref, MOE_TILE)
        xb = x.astype(BF16)
        aff_a = 1.0 / (1.0 + jnp.exp(-jnp.sum(x * rwa_ref[...], axis=-1, keepdims=True)))
        aff_b = 1.0 / (1.0 + jnp.exp(-jnp.sum(x * rwb_ref[...], axis=-1, keepdims=True)))
        tot = aff_a + aff_b
        h_a = (_silu(_dot(xb, wa_in[0])) * _dot(xb, wa_in[1]) * (aff_a / tot)).astype(BF16)
        h_b = (_silu(_dot(xb, wb_in[0])) * _dot(xb, wb_in[1]) * (aff_b / tot)).astype(BF16)
        _rows_to_tiles(o_ref, _dot(h_a, wa_out[...]) + _dot(h_b, wb_out[...]), MOE_TILE)

    @pl.when(pl.program_id(0) >= nvalid_ref[0])
    def _():
        o_ref[...] = jnp.zeros_like(o_ref)


def _expert_mlp(x_sorted, tile_blk, tile_ea, tile_eb, nvalid, rw_rows, wg, wu, wd, layer):
    n_sorted = x_sorted.shape[0] // ROW_SUB
    nt = n_sorted // MOE_TILE
    xmap = lambda i, blk, ea, eb, nv: (blk[i], 0)
    amap = lambda i, blk, ea, eb, nv: (ea[i], 0, 0)
    bmap = lambda i, blk, ea, eb, nv: (eb[i], 0, 0)
    wamap = lambda i, blk, ea, eb, nv: (layer, ea[i], 0, 0)
    wbmap = lambda i, blk, ea, eb, nv: (layer, eb[i], 0, 0)
    w_in = (None, None, D_MODEL, D_FF_EXPERT)
    w_out = (None, None, D_FF_EXPERT, D_MODEL)
    grid_spec = pltpu.PrefetchScalarGridSpec(
        num_scalar_prefetch=4,
        grid=(nt,),
        in_specs=[pl.BlockSpec((MOE_TILE * ROW_SUB, LANES), xmap),
                  pl.BlockSpec((None, 1, D_MODEL), amap), pl.BlockSpec((None, 1, D_MODEL), bmap),
                  pl.BlockSpec(w_in, wamap), pl.BlockSpec(w_in, wamap), pl.BlockSpec(w_out, wamap),
                  pl.BlockSpec(w_in, wbmap), pl.BlockSpec(w_in, wbmap), pl.BlockSpec(w_out, wbmap)],
        out_specs=pl.BlockSpec((MOE_TILE * ROW_SUB, LANES), lambda i, blk, ea, eb, nv: (i, 0)),
        scratch_shapes=[pltpu.VMEM((2, D_MODEL, D_FF_EXPERT), BF16), pltpu.VMEM((D_FF_EXPERT, D_MODEL), BF16),
                        pltpu.VMEM((2, D_MODEL, D_FF_EXPERT), BF16), pltpu.VMEM((D_FF_EXPERT, D_MODEL), BF16)],
    )
    return pl.pallas_call(
        _expert_kernel,
        grid_spec=grid_spec,
        out_shape=jax.ShapeDtypeStruct((n_sorted * ROW_SUB, LANES), F32),
        compiler_params=pltpu.CompilerParams(
            dimension_semantics=("arbitrary",), vmem_limit_bytes=VMEM_LIMIT),
        name="moe_expert_mlp",
    )(tile_blk, tile_ea, tile_eb, nvalid, x_sorted, rw_rows, rw_rows, wg, wu, wd, wg, wu, wd)


def _gather_ln_kernel(pos_ref, pos_next_ref, f_ref, h_ref, g_ref, b_ref, o_ref, buf_ref, sems, *, tm):
    i = pl.program_id(0)
    slot = i % 2

    def fetch(p_ref, s):
        def issue(r2, carry):
            for pri in range(2):
                r = 2 * r2 + pri
                pltpu.make_async_copy(_row_tile(f_ref, p_ref[0, r]), _row_tile(buf_ref.at[s], r),
                                      sems.at[s]).start(priority=pri)
            return carry
        lax.fori_loop(0, tm // 2, issue, 0)

    @pl.when(i == 0)
    def _():
        fetch(pos_ref, slot)

    @pl.when(i + 1 < pl.num_programs(0))
    def _():
        fetch(pos_next_ref, 1 - slot)

    pltpu.make_async_copy(f_ref.at[pl.ds(0, tm * ROW_SUB)], buf_ref.at[slot], sems.at[slot]).wait()
    f = _tiles_to_rows(buf_ref.at[slot], tm)
    o_ref[...] = _layer_norm(DEEPNORM_ALPHA * h_ref[...] + f, g_ref[...], b_ref[...])


def _gather_ln(f_sorted, pos, h, g, b):
    n = h.shape[0]
    tm = min(ROW_TILE, n)
    nt = n // tm
    kern = functools.partial(_gather_ln_kernel, tm=tm)
    pos3 = pos.reshape(nt, 1, tm)
    return pl.pallas_call(
        kern,
        grid=(nt,),
        in_specs=[pl.BlockSpec((None, 1, tm), lambda i: (i, 0, 0), memory_space=pltpu.SMEM),
                  pl.BlockSpec((None, 1, tm), lambda i: (jnp.minimum(i + 1, nt - 1), 0, 0), memory_space=pltpu.SMEM),
                  pl.BlockSpec(memory_space=pl.ANY),
                  pl.BlockSpec((tm, D_MODEL), lambda i: (i, 0)),
                  _resident(g.shape), _resident(b.shape)],
        out_specs=pl.BlockSpec((tm, D_MODEL), lambda i: (i, 0)),
        out_shape=jax.ShapeDtypeStruct((n, D_MODEL), F32),
        scratch_shapes=[pltpu.VMEM((2, tm * ROW_SUB, LANES), F32), pltpu.SemaphoreType.DMA((2,))],
        compiler_params=pltpu.CompilerParams(
            dimension_semantics=("arbitrary",), vmem_limit_bytes=VMEM_LIMIT),
        name="moe_gather_deepnorm_ln",
    )(pos3, pos3, f_sorted, h, g, b)


def _moe_layer(h, rw_t, rb_col, rw_rows, wg, wu, wd, layer, ln_g, ln_b):
    n = h.shape[0]
    n_tiles = n // MOE_TILE + N_BUCKETS
    n_sorted = n_tiles * MOE_TILE
    bucket, rank, counts = _router(h, rw_t, rb_col)
    tiles_per = (counts + MOE_TILE - 1) // MOE_TILE
    tile_end = jnp.cumsum(tiles_per)
    tile_start = tile_end - tiles_per
    pos = tile_start[bucket] * MOE_TILE + rank
    nvalid = tile_end[-1]
    tail = nvalid + jnp.arange(N_BUCKETS, dtype=I32)
    zstart = jnp.concatenate([jnp.maximum(tile_end - 1, 0), jnp.minimum(tail, n_tiles - 1)]) * MOE_TILE
    zvalid = jnp.concatenate([tiles_per > 0, tail < n_tiles]).astype(I32)
    tidx = jnp.minimum(jnp.arange(n_tiles, dtype=I32), nvalid - 1)
    tb = jnp.sum(tile_end[None, :] <= tidx[:, None], axis=1).astype(I32)
    pair_lo = jnp.array([p[0] for p in _PAIRS], I32)
    pair_hi = jnp.array([p[1] for p in _PAIRS], I32)
    tile_ea = (tb // N_PAIRS) * EXPERTS_PER_GROUP + pair_lo[tb % N_PAIRS]
    tile_eb = (tb // N_PAIRS) * EXPERTS_PER_GROUP + pair_hi[tb % N_PAIRS]
    x_sorted = _scatter_rows(h, pos.astype(I32), zstart.astype(I32), zvalid, n_sorted)
    f_sorted = _expert_mlp(x_sorted, tidx, tile_ea.astype(I32), tile_eb.astype(I32),
                           nvalid.reshape(1).astype(I32), rw_rows, wg, wu, wd, layer)
    return _gather_ln(f_sorted, pos.astype(I32), h, ln_g, ln_b)


def _rot_half_slot(v, lane):
    w = v.shape[1]
    fwd = pltpu.roll(v, MLA_ROPE // 2, 1)
    bwd = pltpu.roll(v, w - MLA_ROPE // 2, 1)
    first = (lane >= MLA_NOPE) & (lane < MLA_NOPE + MLA_ROPE // 2)
    second = (lane >= MLA_NOPE + MLA_ROPE // 2) & (lane < MLA_NOPE + MLA_ROPE)
    return jnp.where(first, -bwd, jnp.where(second, fwd, 0.0))


def _mla_proj_kernel(x_ref, wd_ref, wkr_ref, qn_ref, kvn_ref, wuqt_ref, wuk_ref, wuvt_ref,
                     cos_ref, sin_ref, cost_ref, sint_ref, qt_ref, k_ref, vt_ref, *, tm):
    xb = x_ref[...].astype(BF16)
    down = _dot(xb, wd_ref[...])
    c_q = down[:, 0:MLA_Q_RANK]
    c_kv = down[:, MLA_Q_RANK:MLA_Q_RANK + MLA_KV_RANK]
    c_q = c_q * lax.rsqrt(jnp.mean(c_q * c_q, axis=-1, keepdims=True) + RMS_EPS) * qn_ref[...]
    c_kv = c_kv * lax.rsqrt(jnp.mean(c_kv * c_kv, axis=-1, keepdims=True) + RMS_EPS) * kvn_ref[...]
    c_q = c_q.astype(BF16)
    c_kv = c_kv.astype(BF16)

    lane = lax.broadcasted_iota(I32, (tm, HEAD_SLOT), 1)
    kr = _dot(xb, wkr_ref[...])
    kr = kr * cos_ref[...] + _rot_half_slot(kr, lane) * sin_ref[...]
    kchunk = lax.broadcasted_iota(I32, (tm, HEAD_SLOT), 0) // ATTN_CHUNK
    cidx = lane - MASK_LANE0
    kr = kr + jnp.where((cidx >= 0) & (cidx < tm // ATTN_CHUNK) & (kchunk > cidx), -MASK_BIG, 0.0)
    k_all = _dot(c_kv, wuk_ref[...])
    for h in range(MLA_N_HEADS):
        sl = slice(h * HEAD_SLOT, (h + 1) * HEAD_SLOT)
        k_ref[:, sl] = (k_all[:, sl] + kr).astype(BF16)

    heads_per_dot = 4
    for hb in range(MLA_N_HEADS // heads_per_dot):
        rows = heads_per_dot * MLA_V
        vt = _dot_nt(wuvt_ref[hb * rows:(hb + 1) * rows, :], c_kv)
        for r in range(heads_per_dot):
            vt_ref[hb * heads_per_dot + r] = vt[r * MLA_V:(r + 1) * MLA_V].astype(BF16)

    cos_t = cost_ref[...]
    sin_t = sint_ref[...]
    half = MLA_ROPE // 2
    qscale = math.log2(math.e) / math.sqrt(MLA_NOPE + MLA_ROPE)
    for hb in range(MLA_N_HEADS // heads_per_dot):
        rows = heads_per_dot * HEAD_SLOT
        qt = _dot_nt(wuqt_ref[hb * rows:(hb + 1) * rows, :], c_q)
        for r in range(heads_per_dot):
            blk = qt[r * HEAD_SLOT:(r + 1) * HEAD_SLOT]
            t1 = blk[MLA_NOPE:MLA_NOPE + half]
            t2 = blk[MLA_NOPE + half:MLA_NOPE + MLA_ROPE]
            roped = jnp.concatenate([blk[0:MLA_NOPE], t1 * cos_t - t2 * sin_t, t1 * sin_t + t2 * cos_t,
                                     blk[MLA_NOPE + MLA_ROPE:HEAD_SLOT]], axis=0)
            roped = (roped * qscale).astype(BF16)
            for c in range(tm // ATTN_W):
                qt_ref[hb * heads_per_dot + r, c] = roped[:, c * ATTN_W:(c + 1) * ATTN_W]


def _mla_proj(h2, wd, wkr, qn, kvn, wuqt, wuk, wuvt, cos_s, sin_s, cos_t, sin_t, *, bsz, seq):
    n = bsz * seq
    tm = min(ROW_TILE, seq)
    nt = seq // tm
    row = lambda b, j: (b * nt + j, 0)
    kern = functools.partial(_mla_proj_kernel, tm=tm)
    return pl.pallas_call(
        kern,
        grid=(bsz, nt),
        in_specs=[pl.BlockSpec((tm, D_MODEL), row), _resident(wd.shape), _resident(wkr.shape),
                  _resident(qn.shape), _resident(kvn.shape), _resident(wuqt.shape), _resident(wuk.shape),
                  _resident(wuvt.shape),
                  pl.BlockSpec((tm, HEAD_SLOT), lambda b, j: (j, 0)), pl.BlockSpec((tm, HEAD_SLOT), lambda b, j: (j, 0)),
                  pl.BlockSpec((MLA_ROPE // 2, tm), lambda b, j: (0, j)),
                  pl.BlockSpec((MLA_ROPE // 2, tm), lambda b, j: (0, j))],
        out_specs=[pl.BlockSpec((None, MLA_N_HEADS, tm // ATTN_W, HEAD_SLOT, ATTN_W), lambda b, j: (b, 0, j, 0, 0)),
                   pl.BlockSpec((tm, MLA_N_HEADS * HEAD_SLOT), row),
                   pl.BlockSpec((None, MLA_N_HEADS, None, MLA_V, tm), lambda b, j: (b, 0, j, 0, 0))],
        out_shape=[jax.ShapeDtypeStruct((bsz, MLA_N_HEADS, seq // ATTN_W, HEAD_SLOT, ATTN_W), BF16),
                   jax.ShapeDtypeStruct((n, MLA_N_HEADS * HEAD_SLOT), BF16),
                   jax.ShapeDtypeStruct((bsz, MLA_N_HEADS, nt, MLA_V, tm), BF16)],
        compiler_params=pltpu.CompilerParams(
            dimension_semantics=("arbitrary", "arbitrary"), vmem_limit_bytes=VMEM_LIMIT),
        name="mla_proj_rope",
    )(h2, wd, wkr, qn, kvn, wuqt, wuk, wuvt, cos_s, sin_s, cos_t, sin_t)


def _col_reduce(s, op):
    rows, w = s.shape
    r = op(s.reshape(4, rows // 32, 8, w), axis=1)
    return op(op(r, axis=0), axis=0, keepdims=True)


def _attn_kernel(sq_ref, sj_ref, sd_ref, sf_ref, qt_ref, k_ref, vt_ref, ot_ref,
                 s0_ref, s1_ref, m_ref, l_ref, acc_ref, *, tq, n_sched):
    tk = tq
    w = ATTN_W
    halves = tq // w
    row = lax.broadcasted_iota(I32, (HEAD_SLOT, w), 0) - MASK_LANE0
    qchunk = lax.broadcasted_iota(I32, (HEAD_SLOT, w), 1) // ATTN_CHUNK
    onehot = [jnp.where(row == qchunk + hf * (w // ATTN_CHUNK), 1.0, 0.0).astype(BF16) for hf in range(halves)]

    def scores_into(s_ref, t):
        k_tile = k_ref[pl.ds(pl.multiple_of(sj_ref[t] * tk, tk), tk), :]
        diag = jnp.where(sd_ref[t] > 0, 1.0, 0.0).astype(BF16)
        for hf in range(halves):
            q = qt_ref[sq_ref[t] * halves + hf] + onehot[hf] * diag
            s_ref[hf] = _dot(k_tile, q)

    def update(s_ref, t):
        vt = vt_ref[sj_ref[t]]
        first = sf_ref[t] > 0
        for hf in range(halves):
            s = s_ref[hf]
            m = jnp.where(first, -jnp.inf, m_ref[hf])
            l = jnp.where(first, 0.0, l_ref[hf])
            acc = jnp.where(first, 0.0, acc_ref[hf])
            m_new = jnp.maximum(m, _col_reduce(s, jnp.max))
            p = jnp.exp2(s - m_new)
            corr = jnp.exp2(m - m_new)
            l_new = corr * l + _col_reduce(p, jnp.sum)
            acc_new = corr * acc + _dot(vt, p.astype(BF16))
            m_ref[hf] = m_new
            l_ref[hf] = l_new
            acc_ref[hf] = acc_new
            ot_ref[sq_ref[t] * halves + hf] = (acc_new / l_new).astype(BF16)

    scores_into(s0_ref, 0)

    def pair(p, carry):
        t = 2 * p
        scores_into(s1_ref, t + 1)
        update(s0_ref, t)
        scores_into(s0_ref, jnp.minimum(t + 2, n_sched - 1))
        update(s1_ref, t + 1)
        return carry

    lax.fori_loop(0, n_sched // 2, pair, 0)


def _attention(qt, k, vt, *, bsz, seq):
    tq = min(ROW_TILE, seq)
    nq = seq // tq
    halves = tq // ATTN_W
    sched = [(qi, j, int(j == qi), int(j == 0)) for qi in range(nq) for j in range(qi + 1)]
    if len(sched) % 2:
        sched = sched[:1] + sched
    cols = [jnp.array(c, I32) for c in zip(*sched)]
    kern = functools.partial(_attn_kernel, tq=tq, n_sched=len(sched))
    grid_spec = pltpu.PrefetchScalarGridSpec(
        num_scalar_prefetch=4,
        grid=(bsz, MLA_N_HEADS),
        in_specs=[pl.BlockSpec((None, None, nq * halves, HEAD_SLOT, ATTN_W), lambda b, h, *_: (b, h, 0, 0, 0)),
                  pl.BlockSpec((seq, HEAD_SLOT), lambda b, h, *_: (b, h)),
                  pl.BlockSpec((None, None, nq, MLA_V, tq), lambda b, h, *_: (b, h, 0, 0, 0))],
        out_specs=pl.BlockSpec((None, None, nq * halves, MLA_V, ATTN_W), lambda b, h, *_: (b, h, 0, 0, 0)),
        scratch_shapes=[pltpu.VMEM((halves, tq, ATTN_W), F32), pltpu.VMEM((halves, tq, ATTN_W), F32),
                        pltpu.VMEM((halves, 1, ATTN_W), F32), pltpu.VMEM((halves, 1, ATTN_W), F32),
                        pltpu.VMEM((halves, MLA_V, ATTN_W), F32)],
    )
    return pl.pallas_call(
        kern,
        grid_spec=grid_spec,
        out_shape=jax.ShapeDtypeStruct((bsz, MLA_N_HEADS, nq * halves, MLA_V, ATTN_W), BF16),
        compiler_params=pltpu.CompilerParams(
            dimension_semantics=("arbitrary", "arbitrary"), vmem_limit_bytes=VMEM_LIMIT),
        name="mla_flash_attention",
    )(*cols, qt, k, vt)


def _proj_ln_t_kernel(at_ref, w_ref, res_ref, g_ref, b_ref, o_ref):
    at = jnp.concatenate(
        [jnp.concatenate([at_ref[h, c] for c in range(at_ref.shape[1])], axis=1) for h in range(at_ref.shape[0])],
        axis=0)
    m = _dot_tn(at, w_ref[...])
    o_ref[...] = _layer_norm(DEEPNORM_ALPHA * res_ref[...] + m, g_ref[...], b_ref[...])


def _proj_ln_t(a_t, w, res, g, b, *, bsz, seq):
    tm = min(ROW_TILE, seq)
    nt = seq // tm
    row = lambda bb, j: (bb * nt + j, 0)
    a_block = (None, a_t.shape[1], tm // ATTN_W, a_t.shape[3], ATTN_W)
    return pl.pallas_call(
        _proj_ln_t_kernel,
        grid=(bsz, nt),
        in_specs=[pl.BlockSpec(a_block, lambda bb, j: (bb, 0, j, 0, 0)), _resident(w.shape),
                  pl.BlockSpec((tm, D_MODEL), row), _resident(g.shape), _resident(b.shape)],
        out_specs=pl.BlockSpec((tm, D_MODEL), row),
        out_shape=jax.ShapeDtypeStruct((bsz * seq, D_MODEL), F32),
        compiler_params=pltpu.CompilerParams(
            dimension_semantics=("arbitrary", "arbitrary"), vmem_limit_bytes=VMEM_LIMIT),
        name="outproj_t_deepnorm_ln",
    )(a_t, w, res, g, b)


def _head_slots(w, head_dim, n_heads):
    k = w.shape[0]
    w3 = w.reshape(k, n_heads, head_dim)
    return jnp.pad(w3, ((0, 0), (0, 0), (0, HEAD_SLOT - head_dim))).reshape(k, n_heads * HEAD_SLOT)


def kernel(x, ssd_w_in, ssd_conv_w, ssd_conv_b, ssd_dt_bias, ssd_a_log, ssd_d, ssd_norm_w, ssd_w_out, mla_w_down, mla_q_norm, mla_w_uq, mla_kv_norm, mla_w_ukv, mla_w_out, router_w, router_bias, moe_w_gate, moe_w_up, moe_w_down, ln_mix_g, ln_mix_b, ln_ffn_g, ln_ffn_b):
    bsz, seq, _ = x.shape
    n = bsz * seq
    h = x.reshape(n, D_MODEL)

    rw_t = router_w.T
    rw_rows = rw_t.reshape(N_EXPERTS, 1, D_MODEL)
    rb_col = router_bias.reshape(N_EXPERTS, 1)
    head_of_lane_p = jnp.arange(SSD_D_INNER) // SSD_HEAD_DIM
    e64 = (jnp.arange(LANES)[:, None] == head_of_lane_p[None, :]).astype(BF16)
    e128 = (jnp.arange(LANES)[:, None] == (jnp.arange(SSD_N_HEADS * SSD_L) // SSD_L)[None, :]).astype(BF16)

    for i in range(DEPTH):
        j = i // N_MIXERS
        if i % N_MIXERS == 0:
            w_in = ssd_w_in[j]
            wz = w_in[:, :SSD_D_INNER].astype(BF16)
            wx = w_in[:, SSD_D_INNER:SSD_D_INNER + SSD_CONV_DIM].astype(BF16)
            pad_h = LANES - SSD_N_HEADS
            wdt = jnp.pad(w_in[:, SSD_D_INNER + SSD_CONV_DIM:], ((0, 0), (0, pad_h))).astype(BF16)
            dt_bias = jnp.pad(ssd_dt_bias[j], (0, pad_h)).reshape(1, LANES)
            a_row = jnp.pad(-jnp.exp(ssd_a_log[j]) * math.log2(math.e), (0, pad_h)).reshape(1, LANES)
            d_exp = ssd_d[j][head_of_lane_p].reshape(1, SSD_D_INNER)
            z, xbc, dt = _ssd_inproj(h, wz, wx, wdt, ssd_conv_w[j], ssd_conv_b[j].reshape(1, SSD_CONV_DIM),
                                     dt_bias, bsz=bsz, seq=seq)
            y = _ssd_scan(xbc, z, dt, a_row, d_exp, ssd_norm_w[j].reshape(1, SSD_D_INNER), e128, e64,
                          bsz=bsz, seq=seq)
            h = _proj_ln(y, ssd_w_out[j].astype(BF16), h, ln_mix_g[i].reshape(1, D_MODEL),
                         ln_mix_b[i].reshape(1, D_MODEL))
        else:
            w_down = mla_w_down[j]
            wd = w_down[:, :MLA_Q_RANK + MLA_KV_RANK].astype(BF16)
            wkr = jnp.pad(w_down[:, MLA_Q_RANK + MLA_KV_RANK:],
                          ((0, 0), (MLA_NOPE, HEAD_SLOT - MLA_NOPE - MLA_ROPE))).astype(BF16)
            wuqt = _head_slots(mla_w_uq[j], MLA_NOPE + MLA_ROPE, MLA_N_HEADS).T.astype(BF16)
            wukv = mla_w_ukv[j].reshape(MLA_KV_RANK, MLA_N_HEADS, MLA_NOPE + MLA_V)
            wuk = _head_slots(wukv[:, :, :MLA_NOPE].reshape(MLA_KV_RANK, -1), MLA_NOPE, MLA_N_HEADS).astype(BF16)
            wuvt = wukv[:, :, MLA_NOPE:].reshape(MLA_KV_RANK, -1).T.astype(BF16)
            inv = ROPE_THETA ** (-jnp.arange(0, MLA_ROPE, 2, dtype=F32) / MLA_ROPE)
            ang = jnp.arange(seq, dtype=F32)[:, None] * inv[None, :]
            ones = jnp.ones((seq, MLA_NOPE), F32)
            zeros = jnp.zeros((seq, HEAD_SLOT - MLA_NOPE - MLA_ROPE), F32)
            cos_s = jnp.concatenate([ones, jnp.cos(ang), jnp.cos(ang), zeros], axis=1)
            sin_s = jnp.concatenate([0.0 * ones, jnp.sin(ang), jnp.sin(ang), zeros], axis=1)
            qt, k, vt = _mla_proj(h, wd, wkr, mla_q_norm[j].reshape(1, MLA_Q_RANK),
                                  mla_kv_norm[j].reshape(1, MLA_KV_RANK), wuqt, wuk, wuvt,
                                  cos_s, sin_s, jnp.cos(ang).T, jnp.sin(ang).T, bsz=bsz, seq=seq)
            y_t = _attention(qt, k, vt, bsz=bsz, seq=seq)
            h = _proj_ln_t(y_t, mla_w_out[j].astype(BF16), h, ln_mix_g[i].reshape(1, D_MODEL),
                           ln_mix_b[i].reshape(1, D_MODEL), bsz=bsz, seq=seq)
        h = _moe_layer(h, rw_t, rb_col, rw_rows, moe_w_gate, moe_w_up, moe_w_down, i,
                       ln_ffn_g[i].reshape(1, D_MODEL), ln_ffn_b[i].reshape(1, D_MODEL))
    return h.reshape(bsz, seq, D_MODEL)
```
